```python
import math
import jax, jax.numpy as jnp
from jax import lax
import numpy as np

D_MODEL = 1024
BATCH = 16
SEQ = 2048
DEPTH = 1

SSM_WIDTH = D_MODEL // 2
SSM_GROUP = 16
SSM_GROUPS = SSM_WIDTH // SSM_GROUP
SSM_STATE = 64
N_DIR = 2
DT_MIN = 1e-3
DT_MAX = 1e-1
FFT_WIDTH = D_MODEL // 2
FFT_GROUPS = 4
FFT_GROUP = FFT_WIDTH // FFT_GROUPS
D_FF = 4 * D_MODEL
D_PLE = 256
EPS = 1e-6
IN_WIDTH = SSM_WIDTH + FFT_WIDTH + 2 * D_MODEL

kernel_name = 'hybrid_s5_fnet_gated_encoder_block'


def rms_norm(x, g):
    x32 = x.astype(jnp.float32)
    y = x32 * lax.rsqrt(jnp.mean(x32 * x32, axis=-1, keepdims=True) + EPS)
    return (y * g.astype(jnp.float32)).astype(x.dtype)


def _ssm_combine(left, right):
    a1, b1 = left
    a2, b2 = right
    return (a1 * a2, a2 * b1 + b2)


def s5_scan(u32, lam_re, lam_im, log_dt, b_re, b_im, c_re, c_im, reverse):
    f32 = jnp.float32
    lam = lax.complex(lam_re.astype(f32), lam_im.astype(f32))
    dt = jnp.exp(log_dt.astype(f32))[:, None]
    a_bar = jnp.exp(lam * dt)
    b = lax.complex(b_re.astype(f32), b_im.astype(f32))
    b_bar = ((a_bar - 1.0) / lam)[:, :, None] * b
    bu = jnp.einsum('bsgh,gph->bsgp', u32.astype(jnp.complex64), b_bar)
    a_seq = jnp.broadcast_to(a_bar, (1, u32.shape[1]) + a_bar.shape)
    _, states = lax.associative_scan(_ssm_combine, (a_seq, bu), reverse=reverse, axis=1)
    c = lax.complex(c_re.astype(f32), c_im.astype(f32))
    return jnp.real(jnp.einsum('bsgp,ghp->bsgh', states, c))


def s5_branch(u, lam_re, lam_im, log_dt, b_re, b_im, c_re, c_im, d_skip, w_glu):
    bsz, seq = u.shape[0], u.shape[1]
    f32 = jnp.float32
    u32 = u.astype(f32)
    ug = u32.reshape(bsz, seq, SSM_GROUPS, SSM_GROUP)
    y = (s5_scan(ug, lam_re[0], lam_im[0], log_dt[0], b_re[0], b_im[0], c_re[0], c_im[0], False)
         + s5_scan(ug, lam_re[1], lam_im[1], log_dt[1], b_re[1], b_im[1], c_re[1], c_im[1], True))
    y = y.reshape(bsz, seq, SSM_WIDTH) + d_skip.astype(f32) * u32
    a = jax.nn.gelu(y)
    val, gate = jnp.split(a @ w_glu.astype(f32), 2, axis=-1)
    return (val * jax.nn.sigmoid(gate)).astype(u.dtype)


def fnet_branch(u, w_fnet):
    bsz, seq = u.shape[0], u.shape[1]
    ug = u.astype(jnp.float32).reshape(bsz, seq, FFT_GROUPS, FFT_GROUP)
    mixed = jnp.fft.fft2(ug, axes=(1, 3), norm='ortho').real
    mixed = mixed.reshape(bsz, seq, FFT_WIDTH).astype(u.dtype)
    return mixed @ w_fnet


def setup_inputs(seed: int = 0) -> dict:
    key = jax.random.key(seed)
    ks = jax.random.split(key, 24)
    f32 = jnp.float32
    L = DEPTH

    def nrm(k, shape, scale):
        return jax.random.normal(k, shape, f32) * scale

    n_idx = jnp.arange(SSM_STATE, dtype=f32)
    lam_shape = (L, N_DIR, SSM_GROUPS, SSM_STATE)
    b_shape = (L, N_DIR, SSM_GROUPS, SSM_STATE, SSM_GROUP)
    c_shape = (L, N_DIR, SSM_GROUPS, SSM_GROUP, SSM_STATE)
    return {
        'x': nrm(ks[0], (BATCH, SEQ, D_MODEL), 1.0),
        'p': nrm(ks[1], (DEPTH, BATCH, SEQ, D_PLE), 1.0),
        'g_mix': 1.0 + nrm(ks[2], (L, D_MODEL), 0.02),
        'w_in': nrm(ks[3], (L, D_MODEL, IN_WIDTH), D_MODEL ** -0.5),
        'lam_re': -0.5 + nrm(ks[4], lam_shape, 0.01),
        'lam_im': math.pi * n_idx + nrm(ks[5], lam_shape, 0.01),
        'log_dt': jax.random.uniform(ks[6], (L, N_DIR, SSM_GROUPS), f32, math.log(DT_MIN), math.log(DT_MAX)),
        'b_re': nrm(ks[7], b_shape, (2 * SSM_GROUP) ** -0.5),
        'b_im': nrm(ks[8], b_shape, (2 * SSM_GROUP) ** -0.5),
        'c_re': nrm(ks[9], c_shape, SSM_STATE ** -0.5),
        'c_im': nrm(ks[10], c_shape, SSM_STATE ** -0.5),
        'd_skip': nrm(ks[11], (L, SSM_WIDTH), 1.0),
        'w_glu': nrm(ks[12], (L, SSM_WIDTH, 2 * D_MODEL), SSM_WIDTH ** -0.5),
        'w_fnet': nrm(ks[13], (L, FFT_WIDTH, D_MODEL), FFT_WIDTH ** -0.5),
        'w_out': nrm(ks[14], (L, D_MODEL, D_MODEL), D_MODEL ** -0.5),
        'g_ffn': 1.0 + nrm(ks[15], (L, D_MODEL), 0.02),
        'w_up': nrm(ks[16], (L, D_MODEL, D_FF), D_MODEL ** -0.5),
        'w_down': nrm(ks[17], (L, D_FF, D_MODEL), D_FF ** -0.5),
        'g_ple': 1.0 + nrm(ks[18], (L, D_MODEL), 0.02),
        'w_ple_gate': nrm(ks[19], (L, D_MODEL, D_MODEL), D_MODEL ** -0.5),
        'w_ple': nrm(ks[20], (L, D_PLE, D_MODEL), D_PLE ** -0.5),
        'g_final': 1.0 + nrm(ks[21], (D_MODEL,), 0.02),
    }


def reference(x, p, g_mix, w_in, lam_re, lam_im, log_dt, b_re, b_im, c_re, c_im, d_skip,
              w_glu, w_fnet, w_out, g_ffn, w_up, w_down, g_ple, w_ple_gate, w_ple, g_final):
    split_at = [SSM_WIDTH, SSM_WIDTH + FFT_WIDTH, SSM_WIDTH + FFT_WIDTH + D_MODEL]
    for i in range(DEPTH):
        h = rms_norm(x, g_mix[i])
        proj = h @ w_in[i]
        u_ssm, u_fft, z_a, z_b = jnp.split(proj, split_at, axis=-1)
        y_a = s5_branch(u_ssm, lam_re[i], lam_im[i], log_dt[i], b_re[i], b_im[i],
                        c_re[i], c_im[i], d_skip[i], w_glu[i])
        y_b = fnet_branch(u_fft, w_fnet[i])
        merged = jax.nn.sigmoid(z_a) * y_a + jax.nn.sigmoid(z_b) * y_b
        x = x + merged @ w_out[i]
        h = rms_norm(x, g_ffn[i])
        x = x + jnp.square(jax.nn.relu(h @ w_up[i])) @ w_down[i]
        h = rms_norm(x, g_ple[i])
        x = x + jax.nn.sigmoid(h @ w_ple_gate[i]) * (p[i] @ w_ple[i])
    return rms_norm(x, g_final)
```

```python
import functools
import math

import jax
import jax.numpy as jnp
import numpy as np
from jax import lax
from jax.experimental import pallas as pl
from jax.experimental.pallas import tpu as pltpu

D_MODEL = 1024
SSM_WIDTH = 512
SSM_GROUP = 16
SSM_GROUPS = 32
SSM_STATE = 64
FFT_WIDTH = 512
FFT_GROUPS = 4
FFT_GROUP = 128
D_FF = 4096
D_PLE = 256
EPS = 1e-6

CHUNK = 16
CHUNK_VEC = CHUNK * SSM_GROUP

TM_A = 512
TM_D = 256
VMEM_LIMIT = 56 * 1024 * 1024
S5_VMEM_LIMIT = 60 * 1024 * 1024

F32 = jnp.float32
BF16 = jnp.bfloat16
U32 = jnp.uint32


def _rms(x, g):
    ms = jnp.mean(x * x, axis=-1, keepdims=True)
    return (x * lax.rsqrt(ms + EPS)) * g


def _dot(a, b):
    return jnp.dot(a, b, preferred_element_type=F32)


def _dot_nt(a, b, precision=None):
    return lax.dot_general(a, b, (((1,), (1,)), ((), ())), precision=precision,
                           preferred_element_type=F32)


def _stage_a_kernel(x_ref, g_ref, win_ref, wdft_ref, u_ref, pf_ref, za_ref, zb_ref):
    h = _rms(x_ref[...], g_ref[...]).astype(BF16)
    proj = _dot(h, win_ref[...])
    u_ref[...] = proj[:, :SSM_WIDTH].astype(BF16)
    uf = proj[:, SSM_WIDTH:SSM_WIDTH + FFT_WIDTH].astype(BF16)
    for gi in range(FFT_GROUPS):
        z = _dot(uf[:, gi * FFT_GROUP:(gi + 1) * FFT_GROUP], wdft_ref[...])
        pf_ref[:, gi * FFT_GROUP:(gi + 1) * FFT_GROUP] = z[:, :FFT_GROUP].astype(BF16)
        pf_ref[:, FFT_WIDTH + gi * FFT_GROUP:FFT_WIDTH + (gi + 1) * FFT_GROUP] = (
            z[:, FFT_GROUP:].astype(BF16))
    za_ref[...] = proj[:, 2 * 512:2 * 512 + D_MODEL].astype(BF16)
    zb_ref[...] = proj[:, 2 * 512 + D_MODEL:].astype(BF16)


def _stage_a(x2, g_mix, w_in_bf, w_dft):
    n = x2.shape[0]
    row = lambda w: pl.BlockSpec((TM_A, w), lambda i: (i, 0))
    full = lambda a: pl.BlockSpec(a.shape, lambda i: (0,) * a.ndim)
    return pl.pallas_call(
        _stage_a_kernel,
        grid=(n // TM_A,),
        in_specs=[row(D_MODEL), full(g_mix), full(w_in_bf), full(w_dft)],
        out_specs=[row(SSM_WIDTH), row(2 * FFT_WIDTH), row(D_MODEL), row(D_MODEL)],
        out_shape=[jax.ShapeDtypeStruct((n, SSM_WIDTH), BF16),
                   jax.ShapeDtypeStruct((n, 2 * FFT_WIDTH), BF16),
                   jax.ShapeDtypeStruct((n, D_MODEL), BF16),
                   jax.ShapeDtypeStruct((n, D_MODEL), BF16)],
        compiler_params=pltpu.CompilerParams(
            dimension_semantics=("parallel",), vmem_limit_bytes=VMEM_LIMIT),
        name="stage_a_inproj",
    )(x2, g_mix, w_in_bf, w_dft)


LANES = 128
GROUPS_PER_TILE = LANES // SSM_GROUP
SCAN_GROUPS = 4
RELAYOUT_UNROLL = 4


def _chunk_time_order(slot_shift):
    gpt = GROUPS_PER_TILE
    return [gpt * (pos // gpt) + (pos % gpt - slot_shift) % gpt for pos in range(CHUNK)]


def _group_operators(lam_re, lam_im, log_dt, btr, bti, cr, ci, slot_shift):
    order = _chunk_time_order(slot_shift)
    half = SSM_STATE
    is_f = lax.broadcasted_iota(jnp.int32, (1, 2 * half), 1) < half
    dt = jnp.exp(log_dt)
    lr, li = lam_re * dt, lam_im * dt
    mag = jnp.exp(lr)
    a1r, a1i = mag * jnp.cos(li), mag * jnp.sin(li)
    den = lam_re * lam_re + lam_im * lam_im
    kr = ((a1r - 1.0) * lam_re + a1i * lam_im) / den
    ki = (a1i * lam_re - (a1r - 1.0) * lam_im) / den
    bbr, bbi = kr * btr - ki * bti, kr * bti + ki * btr
    pw = [(jnp.ones_like(a1r), jnp.zeros_like(a1r))]
    for _ in range(CHUNK):
        pr, pi = pw[-1]
        pw.append((pr * a1r - pi * a1i, pr * a1i + pi * a1r))

    def dir_pick(seq, tf, tb):
        return jnp.where(is_f, seq[tf], seq[tb])

    pwr, pwi = [q[0] for q in pw], [q[1] for q in pw]
    m_rows = []
    for t in range(CHUNK):
        ar, ai = dir_pick(pwr, CHUNK - 1 - t, t), dir_pick(pwi, CHUNK - 1 - t, t)
        m_rows.append(jnp.concatenate([bbr * ar - bbi * ai, bbr * ai + bbi * ar], axis=1))
    m_mat = jnp.concatenate([m_rows[t] for t in order], axis=0)

    car = [cr * pwr[k] - ci * pwi[k] for k in range(CHUNK + 1)]
    cai = [cr * pwi[k] + ci * pwr[k] for k in range(CHUNK + 1)]
    n_rows = [jnp.concatenate([dir_pick(car, t + 1, CHUNK - t), -dir_pick(cai, t + 1, CHUNK - t)], axis=1)
              for t in range(CHUNK)]
    nt_mat = jnp.concatenate([n_rows[t] for t in order], axis=0)

    k_rows = []
    for i in range(2 * CHUNK - 1):
        lag = abs(i - (CHUNK - 1))
        if i == CHUNK - 1:
            rr, ii = car[lag], cai[lag]
        else:
            keep = is_f if i > CHUNK - 1 else jnp.logical_not(is_f)
            rr, ii = jnp.where(keep, car[lag], 0.0), jnp.where(keep, cai[lag], 0.0)
        k_rows.append(jnp.concatenate([rr, ii], axis=1))
    k_rows.append(jnp.zeros_like(k_rows[0]))
    ca_seq = jnp.concatenate(k_rows, axis=0)
    lb = jnp.concatenate([bbr, -bbi], axis=1)
    k_seq = _dot_nt(lb, ca_seq, precision=lax.Precision.HIGHEST)
    t_rows = []
    for t in range(CHUNK):
        off = SSM_GROUP * (CHUNK - 1 - t)
        t_rows.append(k_seq[:, off:off + CHUNK_VEC])
    t_mat = jnp.concatenate([t_rows[t] for t in order], axis=0)
    if slot_shift:
        t_mat = jnp.concatenate(
            [pltpu.roll(t_mat[:, k * LANES:(k + 1) * LANES], slot_shift * SSM_GROUP, 1)
             for k in range(CHUNK_VEC // LANES)], axis=1)
    return t_mat.astype(BF16), m_mat.astype(BF16), nt_mat.astype(BF16), pwr[CHUNK], pwi[CHUNK]


def _s5_kernel(u_ref, lre_ref, lim_ref, ldt_ref, btr_ref, bti_ref, cre_ref, cim_ref, y_ref,
               ug_ref, e_ref, x_ref, t_ref, nt_ref, *, n_chunks, bsz):
    half = SSM_STATE
    gpt = GROUPS_PER_TILE
    slot = lax.broadcasted_iota(jnp.int32, (bsz, LANES), 1) // SSM_GROUP
    slot_is = [slot == s for s in range(gpt)]

    def rot(v, slots):
        if slots % gpt == 0:
            return v
        return pltpu.roll(v.astype(F32), (slots % gpt) * SSM_GROUP, 1).astype(BF16)

    def gather_chunk(c, carry):
        base = c * CHUNK
        r = [rot(u_ref[pl.ds(pl.multiple_of((base + t) * bsz, bsz), bsz), :], t) for t in range(CHUNK)]
        for tile in range(CHUNK // gpt):
            for g in range(gpt):
                d = r[tile * gpt + (-g) % gpt]
                for s in range(1, gpt):
                    d = jnp.where(slot_is[s], r[tile * gpt + (s - g) % gpt], d)
                ug_ref[g, pl.ds(pl.multiple_of(c * bsz, bsz), bsz), tile * LANES:(tile + 1) * LANES] = d
        return carry

    lax.fori_loop(0, n_chunks, gather_chunk, 0, unroll=RELAYOUT_UNROLL)

    is_fwd = lax.broadcasted_iota(jnp.int32, (bsz, 2 * half), 1) < half
    for first in range(0, gpt, SCAN_GROUPS):
        a16 = []
        for k in range(SCAN_GROUPS):
            g = first + k
            t_mat, m_mat, nt_mat, ar, ai = _group_operators(
                lre_ref[g], lim_ref[g], ldt_ref[g], btr_ref[g], bti_ref[g], cre_ref[g], cim_ref[g], g)
            t_ref[k] = t_mat
            nt_ref[k] = nt_mat
            a16.append((ar, ai))
            e_ref[k] = _dot(ug_ref[g], m_mat)

        def scan_step(i, carry):
            rf = pl.multiple_of(i * bsz, bsz)
            rb = pl.multiple_of((n_chunks - 1 - i) * bsz, bsz)
            out = []
            for k in range(SCAN_GROUPS):
                xr, xi = carry[k]
                ar, ai = a16[k]
                xrb, xib = xr.astype(BF16), xi.astype(BF16)
                x_ref[k, pl.ds(rf, bsz), 0:half] = xrb[:, 0:half]
                x_ref[k, pl.ds(rb, bsz), half:2 * half] = xrb[:, half:]
                x_ref[k, pl.ds(rf, bsz), 2 * half:3 * half] = xib[:, 0:half]
                x_ref[k, pl.ds(rb, bsz), 3 * half:] = xib[:, half:]
                er = jnp.where(is_fwd, e_ref[k, pl.ds(rf, bsz), 0:2 * half],
                               e_ref[k, pl.ds(rb, bsz), 0:2 * half])
                ei = jnp.where(is_fwd, e_ref[k, pl.ds(rf, bsz), 2 * half:],
                               e_ref[k, pl.ds(rb, bsz), 2 * half:])
                out.append((ar * xr - ai * xi + er, ar * xi + ai * xr + ei))
            return tuple(out)

        zero = jnp.zeros((bsz, 2 * half), F32)
        lax.fori_loop(0, n_chunks, scan_step, tuple((zero, zero) for _ in range(SCAN_GROUPS)))

        for k in range(SCAN_GROUPS):
            g = first + k
            y = _dot(ug_ref[g], t_ref[k]) + _dot_nt(x_ref[k], nt_ref[k])
            ug_ref[g] = y.astype(BF16)

    def scatter_chunk(c, carry):
        rows = pl.ds(pl.multiple_of(c * bsz, bsz), bsz)
        base = c * CHUNK
        for tile in range(CHUNK // gpt):
            d = [ug_ref[g, rows, tile * LANES:(tile + 1) * LANES] for g in range(gpt)]
            for i in range(gpt):
                w = d[(-i) % gpt]
                for s in range(1, gpt):
                    w = jnp.where(slot_is[s], d[(s - i) % gpt], w)
                t = tile * gpt + i
                y_ref[pl.ds(pl.multiple_of((base + t) * bsz, bsz), bsz), :] = rot(w, -i)
        return carry

    lax.fori_loop(0, n_chunks, scatter_chunk, 0, unroll=RELAYOUT_UNROLL)


def _s5_call(u_sb, params, n_chunks, bsz):
    n_tok, width = u_sb.shape
    rows = n_chunks * bsz
    gpt = GROUPS_PER_TILE
    pspec = lambda a: pl.BlockSpec((gpt,) + a.shape[1:], lambda j: (j, 0, 0))
    return pl.pallas_call(
        functools.partial(_s5_kernel, n_chunks=n_chunks, bsz=bsz),
        grid=(width // LANES,),
        in_specs=[pl.BlockSpec((n_tok, LANES), lambda j: (0, j))] + [pspec(a) for a in params],
        out_specs=pl.BlockSpec((n_tok, LANES), lambda j: (0, j), pipeline_mode=pl.Buffered(1)),
        out_shape=jax.ShapeDtypeStruct((n_tok, width), BF16),
        scratch_shapes=[pltpu.VMEM((gpt, rows, CHUNK_VEC), BF16),
                        pltpu.VMEM((SCAN_GROUPS, rows, 4 * SSM_STATE), F32),
                        pltpu.VMEM((SCAN_GROUPS, rows, 4 * SSM_STATE), BF16),
                        pltpu.VMEM((SCAN_GROUPS, CHUNK_VEC, CHUNK_VEC), BF16),
                        pltpu.VMEM((SCAN_GROUPS, CHUNK_VEC, CHUNK_VEC), BF16)],
        compiler_params=pltpu.CompilerParams(
            dimension_semantics=("parallel",), vmem_limit_bytes=S5_VMEM_LIMIT),
        name="s5_chunked_scan",
    )(u_sb, *params)


def _s5_params(lam_re, lam_im, log_dt, b_re, b_im, c_re, c_im):
    cat = lambda a: jnp.concatenate([a[0], a[1]], axis=-1)
    g = lam_re.shape[1]
    ldt = jnp.broadcast_to(log_dt[:, :, None], (2, g, SSM_STATE))
    return [cat(lam_re)[:, None, :], cat(lam_im)[:, None, :], cat(ldt)[:, None, :],
            cat(b_re.transpose(0, 1, 3, 2)), cat(b_im.transpose(0, 1, 3, 2)), cat(c_re), cat(c_im)]


def _fnet_kernel(p_ref, cos_ref, sin_ref, o_ref):
    p = p_ref[0]
    o = _dot(cos_ref[...], p[:, :FFT_WIDTH]) + _dot(sin_ref[...], p[:, FFT_WIDTH:])
    o_ref[0] = o.astype(BF16)


def _fnet_call(pf3, cos_t, sin_t):
    b, s, _ = pf3.shape
    const = lambda a: pl.BlockSpec(a.shape, lambda i: (0, 0), pipeline_mode=pl.Buffered(1))
    return pl.pallas_call(
        _fnet_kernel,
        grid=(b,),
        in_specs=[pl.BlockSpec((1, s, 2 * FFT_WIDTH), lambda i: (i, 0, 0)), const(cos_t), const(sin_t)],
        out_specs=pl.BlockSpec((1, s, FFT_WIDTH), lambda i: (i, 0, 0)),
        out_shape=jax.ShapeDtypeStruct((b, s, FFT_WIDTH), BF16),
        compiler_params=pltpu.CompilerParams(
            dimension_semantics=("parallel",), vmem_limit_bytes=VMEM_LIMIT),
        name="fnet_seq_dft",
    )(pf3, cos_t, sin_t)


@functools.lru_cache(maxsize=None)
def _dft_tables(seq):
    c = np.arange(FFT_GROUP)
    ang = 2.0 * np.pi * np.outer(c, c) / FFT_GROUP
    w_ch = np.concatenate([np.cos(ang), -np.sin(ang)], axis=1) / math.sqrt(FFT_GROUP)
    m = np.arange(seq)
    ang = 2.0 * np.pi * (np.outer(m, m) % seq) / seq
    scale = 1.0 / math.sqrt(seq)
    return (w_ch.astype(np.float32), (np.cos(ang) * scale).astype(np.float32),
            (np.sin(ang) * scale).astype(np.float32))


def _stage_d_kernel(x_ref, u_ref, y_ref, mx_ref, za_ref, zb_ref, p_ref,
                    dskip_ref, wglu_ref, wfnet_ref, wout_ref, gffn_ref, wup_ref, wdown_ref,
                    gple_ref, wgate_ref, wple_ref, gfin_ref, o_ref, *, final_norm):
    x = x_ref[...]
    s = y_ref[...].astype(F32) + dskip_ref[...] * u_ref[...].astype(F32)
    a = 0.5 * s * (1.0 + jnp.tanh(math.sqrt(2.0 / math.pi) * (s + 0.044715 * (s * s * s))))
    vg = _dot(a.astype(BF16), wglu_ref[...])
    y_a = vg[:, :D_MODEL] * jax.nn.sigmoid(vg[:, D_MODEL:])
    y_b = _dot(mx_ref[...], wfnet_ref[...])
    merged = (jax.nn.sigmoid(za_ref[...].astype(F32)) * y_a
              + jax.nn.sigmoid(zb_ref[...].astype(F32)) * y_b)
    x = x + _dot(merged.astype(BF16), wout_ref[...])
    h = _rms(x, gffn_ref[...]).astype(BF16)
    up = jnp.maximum(_dot(h, wup_ref[...]), 0.0)
    x = x + _dot((up * up).astype(BF16), wdown_ref[...])
    h = _rms(x, gple_ref[...]).astype(BF16)
    gate = jax.nn.sigmoid(_dot(h, wgate_ref[...]))
    x = x + gate * _dot(p_ref[...].astype(BF16), wple_ref[...])
    o_ref[...] = _rms(x, gfin_ref[...]) if final_norm else x


def _stage_d(x2, u, y, mx, za, zb, p2, consts, final_norm):
    n = x2.shape[0]
    row = lambda w: pl.BlockSpec((TM_D, w), lambda i: (i, 0))
    const = lambda a: pl.BlockSpec(a.shape, lambda i: (0, 0), pipeline_mode=pl.Buffered(1))
    return pl.pallas_call(
        functools.partial(_stage_d_kernel, final_norm=final_norm),
        grid=(n // TM_D,),
        in_specs=[row(D_MODEL), row(SSM_WIDTH), row(SSM_WIDTH), row(FFT_WIDTH),
                  row(D_MODEL), row(D_MODEL), row(D_PLE)] + [const(c) for c in consts],
        out_specs=row(D_MODEL),
        out_shape=jax.ShapeDtypeStruct((n, D_MODEL), F32),
        compiler_params=pltpu.CompilerParams(
            dimension_semantics=("parallel",), vmem_limit_bytes=VMEM_LIMIT),
        name="stage_d_merge_mlp",
    )(x2, u, y, mx, za, zb, p2, *consts)


def kernel(x, p, g_mix, w_in, lam_re, lam_im, log_dt, b_re, b_im, c_re, c_im, d_skip,
           w_glu, w_fnet, w_out, g_ffn, w_up, w_down, g_ple, w_ple_gate, w_ple, g_final):
    depth = w_in.shape[0]
    bsz, seq, d = x.shape
    assert d == D_MODEL and seq % CHUNK == 0 and (bsz * seq) % TM_A == 0
    assert bsz == 2 * GROUPS_PER_TILE
    n_chunks = seq // CHUNK
    n = bsz * seq
    w_ch, cos_t, sin_t = (jnp.asarray(t).astype(BF16) for t in _dft_tables(seq))
    vec = lambda v: v.reshape(1, -1).astype(F32)

    x2 = x.reshape(n, d)
    for i in range(depth):
        u, pf, za, zb = _stage_a(x2, vec(g_mix[i]), w_in[i].astype(BF16), w_ch)

        u_sb = u.reshape(bsz, seq, SSM_WIDTH).transpose(1, 0, 2).reshape(n, SSM_WIDTH)
        y_sb = _s5_call(u_sb, _s5_params(lam_re[i], lam_im[i], log_dt[i], b_re[i], b_im[i],
                                         c_re[i], c_im[i]), n_chunks, bsz)
        y = y_sb.reshape(seq, bsz, SSM_WIDTH).transpose(1, 0, 2).reshape(n, SSM_WIDTH)

        mx = _fnet_call(pf.reshape(bsz, seq, 2 * FFT_WIDTH), cos_t, sin_t).reshape(n, FFT_WIDTH)

        consts = [vec(d_skip[i]), w_glu[i].astype(BF16), w_fnet[i].astype(BF16),
                  w_out[i].astype(BF16), vec(g_ffn[i]), w_up[i].astype(BF16),
                  w_down[i].astype(BF16), vec(g_ple[i]), w_ple_gate[i].astype(BF16),
                  w_ple[i].astype(BF16), vec(g_final)]
        x2 = _stage_d(x2, u, y, mx, za, zb, p[i].reshape(n, D_PLE), consts, i == depth - 1)
    return x2.reshape(bsz, seq, d)
```

```python
import functools
import math

import jax
import jax.numpy as jnp
import numpy as np
from jax import lax
from jax.experimental import pallas as pl
from jax.experimental.pallas import tpu as pltpu

D_MODEL = 1024
SSM_WIDTH = 512
SSM_GROUP = 16
SSM_GROUPS = 32
SSM_STATE = 64
FFT_WIDTH = 512
FFT_GROUPS = 4
FFT_GROUP = 128
D_FF = 4096
D_PLE = 256
EPS = 1e-6

CHUNK = 16
CHUNK_VEC = CHUNK * SSM_GROUP
SEQ_RADIX = 8
LANES = 128

TM_A = 512
TM_D = 512
VMEM_LIMIT = 56 * 1024 * 1024
S5_VMEM_LIMIT = 60 * 1024 * 1024

F32 = jnp.float32
BF16 = jnp.bfloat16
U32 = jnp.uint32


def _rms(x, g):
    ms = jnp.mean(x * x, axis=-1, keepdims=True)
    return (x * lax.rsqrt(ms + EPS)) * g


def _dot(a, b):
    return jnp.dot(a, b, preferred_element_type=F32)


def _dot_nt(a, b, precision=None):
    return lax.dot_general(a, b, (((1,), (1,)), ((), ())), precision=precision,
                           preferred_element_type=F32)


def _stage_a_kernel(x_ref, g_ref, win_ref, wdft_ref, u_ref, pf_ref, za_ref, zb_ref, z_ref):
    h = _rms(x_ref[...], g_ref[...]).astype(BF16)
    proj = _dot(h, win_ref[...])
    u_ref[...] = proj[:, :SSM_WIDTH].astype(BF16)
    uf = proj[:, SSM_WIDTH:SSM_WIDTH + FFT_WIDTH].astype(BF16)
    for gi in range(FFT_GROUPS):
        z = _dot(uf[:, gi * FFT_GROUP:(gi + 1) * FFT_GROUP], wdft_ref[...])
        z_ref[gi] = z[:, :FFT_GROUP]
        z_ref[FFT_GROUPS + gi] = z[:, FFT_GROUP:]
    for s2 in range(SEQ_RADIX):
        for part in range(2):
            for gi in range(FFT_GROUPS):
                zs = z_ref[part * FFT_GROUPS + gi, pl.ds(s2, TM_A // SEQ_RADIX, stride=SEQ_RADIX), :]
                pf_ref[0, s2, part, :, gi * FFT_GROUP:(gi + 1) * FFT_GROUP] = zs.astype(BF16)
    za_ref[...] = proj[:, 2 * 512:2 * 512 + D_MODEL].astype(BF16)
    zb_ref[...] = proj[:, 2 * 512 + D_MODEL:].astype(BF16)


def _stage_a(x2, g_mix, w_in_bf, w_dft, bsz, seq):
    n = x2.shape[0]
    tiles_per_seq = seq // TM_A
    seq_sub = seq // SEQ_RADIX
    row = lambda w: pl.BlockSpec((TM_A, w), lambda i: (i, 0))
    full = lambda a: pl.BlockSpec(a.shape, lambda i: (0,) * a.ndim)
    pf_spec = pl.BlockSpec((1, SEQ_RADIX, 2, TM_A // SEQ_RADIX, FFT_WIDTH),
                           lambda i: (i // tiles_per_seq, 0, 0, i % tiles_per_seq, 0))
    return pl.pallas_call(
        _stage_a_kernel,
        grid=(n // TM_A,),
        in_specs=[row(D_MODEL), full(g_mix), full(w_in_bf), full(w_dft)],
        out_specs=[row(SSM_WIDTH), pf_spec, row(D_MODEL), row(D_MODEL)],
        out_shape=[jax.ShapeDtypeStruct((n, SSM_WIDTH), BF16),
                   jax.ShapeDtypeStruct((bsz, SEQ_RADIX, 2, seq_sub, FFT_WIDTH), BF16),
                   jax.ShapeDtypeStruct((n, D_MODEL), BF16),
                   jax.ShapeDtypeStruct((n, D_MODEL), BF16)],
        scratch_shapes=[pltpu.VMEM((2 * FFT_GROUPS, TM_A, FFT_GROUP), F32)],
        compiler_params=pltpu.CompilerParams(
            dimension_semantics=("parallel",), vmem_limit_bytes=VMEM_LIMIT),
        name="stage_a_inproj",
    )(x2, g_mix, w_in_bf, w_dft)


GROUPS_PER_TILE = LANES // SSM_GROUP
SCAN_GROUPS = 4
RELAYOUT_UNROLL = 4


def _chunk_time_order(slot_shift):
    gpt = GROUPS_PER_TILE
    return [gpt * (pos // gpt) + (pos % gpt - slot_shift) % gpt for pos in range(CHUNK)]


def _group_operators(lam_re, lam_im, log_dt, btr, bti, cr, ci, slot_shift):
    order = _chunk_time_order(slot_shift)
    half = SSM_STATE
    is_f = lax.broadcasted_iota(jnp.int32, (1, 2 * half), 1) < half
    dt = jnp.exp(log_dt)
    lr, li = lam_re * dt, lam_im * dt
    mag = jnp.exp(lr)
    a1r, a1i = mag * jnp.cos(li), mag * jnp.sin(li)
    den = lam_re * lam_re + lam_im * lam_im
    kr = ((a1r - 1.0) * lam_re + a1i * lam_im) / den
    ki = (a1i * lam_re - (a1r - 1.0) * lam_im) / den
    bbr, bbi = kr * btr - ki * bti, kr * bti + ki * btr
    pw = [(jnp.ones_like(a1r), jnp.zeros_like(a1r))]
    for _ in range(CHUNK):
        pr, pi = pw[-1]
        pw.append((pr * a1r - pi * a1i, pr * a1i + pi * a1r))

    def dir_pick(seq, tf, tb):
        return jnp.where(is_f, seq[tf], seq[tb])

    pwr, pwi = [q[0] for q in pw], [q[1] for q in pw]
    m_rows = []
    for t in range(CHUNK):
        ar, ai = dir_pick(pwr, CHUNK - 1 - t, t), dir_pick(pwi, CHUNK - 1 - t, t)
        m_rows.append(jnp.concatenate([bbr * ar - bbi * ai, bbr * ai + bbi * ar], axis=1))
    m_mat = jnp.concatenate([m_rows[t] for t in order], axis=0)

    car = [cr * pwr[k] - ci * pwi[k] for k in range(CHUNK + 1)]
    cai = [cr * pwi[k] + ci * pwr[k] for k in range(CHUNK + 1)]
    n_rows = [jnp.concatenate([dir_pick(car, t + 1, CHUNK - t), -dir_pick(cai, t + 1, CHUNK - t)], axis=1)
              for t in range(CHUNK)]
    nt_mat = jnp.concatenate([n_rows[t] for t in order], axis=0)

    k_rows = []
    for i in range(2 * CHUNK - 1):
        lag = abs(i - (CHUNK - 1))
        if i == CHUNK - 1:
            rr, ii = car[lag], cai[lag]
        else:
            keep = is_f if i > CHUNK - 1 else jnp.logical_not(is_f)
            rr, ii = jnp.where(keep, car[lag], 0.0), jnp.where(keep, cai[lag], 0.0)
        k_rows.append(jnp.concatenate([rr, ii], axis=1))
    k_rows.append(jnp.zeros_like(k_rows[0]))
    ca_seq = jnp.concatenate(k_rows, axis=0)
    lb = jnp.concatenate([bbr, -bbi], axis=1)
    k_seq = _dot_nt(lb, ca_seq, precision=lax.Precision.HIGHEST)
    t_rows = []
    for t in range(CHUNK):
        off = SSM_GROUP * (CHUNK - 1 - t)
        t_rows.append(k_seq[:, off:off + CHUNK_VEC])
    t_mat = jnp.concatenate([t_rows[t] for t in order], axis=0)
    if slot_shift:
        t_mat = jnp.concatenate(
            [pltpu.roll(t_mat[:, k * LANES:(k + 1) * LANES], slot_shift * SSM_GROUP, 1)
             for k in range(CHUNK_VEC // LANES)], axis=1)
    return t_mat.astype(BF16), m_mat.astype(BF16), nt_mat.astype(BF16), pwr[CHUNK], pwi[CHUNK]


def _s5_kernel(u_ref, lre_ref, lim_ref, ldt_ref, btr_ref, bti_ref, cre_ref, cim_ref, y_ref,
               ug_ref, e_ref, x_ref, t_ref, nt_ref, *, n_chunks, bsz):
    half = SSM_STATE
    gpt = GROUPS_PER_TILE
    slot = lax.broadcasted_iota(jnp.int32, (bsz, LANES), 1) // SSM_GROUP
    slot_is = [slot == s for s in range(gpt)]

    def rot(v, slots):
        if slots % gpt == 0:
            return v
        return pltpu.roll(v.astype(F32), (slots % gpt) * SSM_GROUP, 1).astype(BF16)

    def gather_chunk(c, carry):
        base = c * CHUNK
        r = [rot(u_ref[pl.ds(pl.multiple_of((base + t) * bsz, bsz), bsz), :], t) for t in range(CHUNK)]
        for tile in range(CHUNK // gpt):
            for g in range(gpt):
                d = r[tile * gpt + (-g) % gpt]
                for s in range(1, gpt):
                    d = jnp.where(slot_is[s], r[tile * gpt + (s - g) % gpt], d)
                ug_ref[g, pl.ds(pl.multiple_of(c * bsz, bsz), bsz), tile * LANES:(tile + 1) * LANES] = d
        return carry

    lax.fori_loop(0, n_chunks, gather_chunk, 0, unroll=RELAYOUT_UNROLL)

    is_fwd = lax.broadcasted_iota(jnp.int32, (bsz, 2 * half), 1) < half
    for first in range(0, gpt, SCAN_GROUPS):
        a16 = []
        for k in range(SCAN_GROUPS):
            g = first + k
            t_mat, m_mat, nt_mat, ar, ai = _group_operators(
                lre_ref[g], lim_ref[g], ldt_ref[g], btr_ref[g], bti_ref[g], cre_ref[g], cim_ref[g], g)
            t_ref[k] = t_mat
            nt_ref[k] = nt_mat
            a16.append((ar, ai))
            e_ref[k] = _dot(ug_ref[g], m_mat)

        def scan_step(i, carry):
            rf = pl.multiple_of(i * bsz, bsz)
            rb = pl.multiple_of((n_chunks - 1 - i) * bsz, bsz)
            out = []
            for k in range(SCAN_GROUPS):
                xr, xi = carry[k]
                ar, ai = a16[k]
                xrb, xib = xr.astype(BF16), xi.astype(BF16)
                x_ref[k, pl.ds(rf, bsz), 0:half] = xrb[:, 0:half]
                x_ref[k, pl.ds(rb, bsz), half:2 * half] = xrb[:, half:]
                x_ref[k, pl.ds(rf, bsz), 2 * half:3 * half] = xib[:, 0:half]
                x_ref[k, pl.ds(rb, bsz), 3 * half:] = xib[:, half:]
                er = jnp.where(is_fwd, e_ref[k, pl.ds(rf, bsz), 0:2 * half],
                               e_ref[k, pl.ds(rb, bsz), 0:2 * half])
                ei = jnp.where(is_fwd, e_ref[k, pl.ds(rf, bsz), 2 * half:],
                               e_ref[k, pl.ds(rb, bsz), 2 * half:])
                out.append((ar * xr - ai * xi + er, ar * xi + ai * xr + ei))
            return tuple(out)

        zero = jnp.zeros((bsz, 2 * half), F32)
        lax.fori_loop(0, n_chunks, scan_step, tuple((zero, zero) for _ in range(SCAN_GROUPS)))

        for k in range(SCAN_GROUPS):
            g = first + k
            y = _dot(ug_ref[g], t_ref[k]) + _dot_nt(x_ref[k], nt_ref[k])
            ug_ref[g] = y.astype(BF16)

    def scatter_chunk(c, carry):
        rows = pl.ds(pl.multiple_of(c * bsz, bsz), bsz)
        base = c * CHUNK
        for tile in range(CHUNK // gpt):
            d = [ug_ref[g, rows, tile * LANES:(tile + 1) * LANES] for g in range(gpt)]
            for i in range(gpt):
                w = d[(-i) % gpt]
                for s in range(1, gpt):
                    w = jnp.where(slot_is[s], d[(s - i) % gpt], w)
                t = tile * gpt + i
                y_ref[pl.ds(pl.multiple_of((base + t) * bsz, bsz), bsz), :] = rot(w, -i)
        return carry

    lax.fori_loop(0, n_chunks, scatter_chunk, 0, unroll=RELAYOUT_UNROLL)


def _s5_call(u_sb, params, n_chunks, bsz):
    n_tok, width = u_sb.shape
    rows = n_chunks * bsz
    gpt = GROUPS_PER_TILE
    pspec = lambda a: pl.BlockSpec((gpt,) + a.shape[1:], lambda j: (j, 0, 0))
    return pl.pallas_call(
        functools.partial(_s5_kernel, n_chunks=n_chunks, bsz=bsz),
        grid=(width // LANES,),
        in_specs=[pl.BlockSpec((n_tok, LANES), lambda j: (0, j))] + [pspec(a) for a in params],
        out_specs=pl.BlockSpec((n_tok, LANES), lambda j: (0, j), pipeline_mode=pl.Buffered(1)),
        out_shape=jax.ShapeDtypeStruct((n_tok, width), BF16),
        scratch_shapes=[pltpu.VMEM((gpt, rows, CHUNK_VEC), BF16),
                        pltpu.VMEM((SCAN_GROUPS, rows, 4 * SSM_STATE), F32),
                        pltpu.VMEM((SCAN_GROUPS, rows, 4 * SSM_STATE), BF16),
                        pltpu.VMEM((SCAN_GROUPS, CHUNK_VEC, CHUNK_VEC), BF16),
                        pltpu.VMEM((SCAN_GROUPS, CHUNK_VEC, CHUNK_VEC), BF16)],
        compiler_params=pltpu.CompilerParams(
            dimension_semantics=("parallel",), vmem_limit_bytes=S5_VMEM_LIMIT),
        name="s5_chunked_scan",
    )(u_sb, *params)


def _s5_params(lam_re, lam_im, log_dt, b_re, b_im, c_re, c_im):
    cat = lambda a: jnp.concatenate([a[0], a[1]], axis=-1)
    g = lam_re.shape[1]
    ldt = jnp.broadcast_to(log_dt[:, :, None], (2, g, SSM_STATE))
    return [cat(lam_re)[:, None, :], cat(lam_im)[:, None, :], cat(ldt)[:, None, :],
            cat(b_re.transpose(0, 1, 3, 2)), cat(b_im.transpose(0, 1, 3, 2)), cat(c_re), cat(c_im)]


def _dft8_real(a):
    def dft4(b):
        (a0r, a0i), (a1r, a1i), (a2r, a2i), (a3r, a3i) = b
        t0r, t0i, t1r, t1i = a0r + a2r, a0i + a2i, a0r - a2r, a0i - a2i
        t2r, t2i, t3r, t3i = a1r + a3r, a1i + a3i, a1r - a3r, a1i - a3i
        return [(t0r + t2r, t0i + t2i), (t1r + t3i, t1i - t3r), (t0r - t2r, t0i - t2i), (t1r - t3i, t1i + t3r)]

    ev, od = dft4(a[0::2]), dft4(a[1::2])
    rt = math.sqrt(0.5)
    p = [od[0][0], rt * (od[1][0] + od[1][1]), od[2][1], rt * (od[3][1] - od[3][0])]
    return [ev[k][0] + p[k] for k in range(4)] + [ev[k][0] - p[k] for k in range(4)]


def _fnet_kernel(p_ref, w1_ref, twc_ref, tws_ref, o_ref):
    sub = w1_ref.shape[0] // 2
    width = 2 * LANES
    for j in range(FFT_WIDTH // width):
        cols = slice(j * width, (j + 1) * width)
        a = []
        for s2 in range(SEQ_RADIX):
            y = _dot(w1_ref[...], p_ref[0, s2, :, :, cols].reshape(2 * sub, width))
            yr, yi = y[:sub], y[sub:]
            if s2:
                c = jnp.concatenate([twc_ref[s2]] * (width // LANES), axis=1)
                s = jnp.concatenate([tws_ref[s2]] * (width // LANES), axis=1)
                yr, yi = c * yr + s * yi, c * yi - s * yr
            a.append((yr, yi))
        out = _dft8_real(a)
        for m2 in range(SEQ_RADIX):
            o_ref[0, m2 * sub:(m2 + 1) * sub, cols] = out[m2].astype(BF16)


def _fnet_call(pf5, w1, twc, tws):
    b, radix, _, sub, width = pf5.shape
    const = lambda a: pl.BlockSpec(a.shape, lambda i: (0,) * a.ndim)
    return pl.pallas_call(
        _fnet_kernel,
        grid=(b,),
        in_specs=[pl.BlockSpec((1, radix, 2, sub, width), lambda i: (i, 0, 0, 0, 0)),
                  const(w1), const(twc), const(tws)],
        out_specs=pl.BlockSpec((1, radix * sub, width), lambda i: (i, 0, 0)),
        out_shape=jax.ShapeDtypeStruct((b, radix * sub, width), BF16),
        compiler_params=pltpu.CompilerParams(
            dimension_semantics=("parallel",), vmem_limit_bytes=VMEM_LIMIT),
        name="fnet_seq_dft",
    )(pf5, w1, twc, tws)


@functools.lru_cache(maxsize=None)
def _dft_tables(seq):
    c = np.arange(FFT_GROUP)
    ang = 2.0 * np.pi * np.outer(c, c) / FFT_GROUP
    w_ch = np.concatenate([np.cos(ang), -np.sin(ang)], axis=1) / math.sqrt(FFT_GROUP)
    sub = seq // SEQ_RADIX
    m = np.arange(sub)
    ang = 2.0 * np.pi * (np.outer(m, m) % sub) / sub
    cs, sn = np.cos(ang) / math.sqrt(seq), np.sin(ang) / math.sqrt(seq)
    w1 = np.block([[cs, sn], [-sn, cs]])
    ang = 2.0 * np.pi * np.outer(np.arange(SEQ_RADIX), m) / seq
    rep = lambda t: np.repeat(t[:, :, None], LANES, axis=2).astype(np.float32)
    return w_ch.astype(np.float32), w1.astype(np.float32), rep(np.cos(ang)), rep(np.sin(ang))


def _stage_d_kernel(x_ref, u_ref, y_ref, mx_ref, za_ref, zb_ref, p_ref,
                    dskip_ref, wglu_ref, wfnet_ref, wout_ref, gffn_ref, wup_ref, wdown_ref,
                    gple_ref, wgate_ref, wple_ref, gfin_ref, o_ref, *, final_norm):
    x = x_ref[...]
    s = y_ref[...].astype(F32) + dskip_ref[...] * u_ref[...].astype(F32)
    a = 0.5 * s * (1.0 + jnp.tanh(math.sqrt(2.0 / math.pi) * (s + 0.044715 * (s * s * s))))
    vg = _dot(a.astype(BF16), wglu_ref[...])
    y_a = vg[:, :D_MODEL] * jax.nn.sigmoid(vg[:, D_MODEL:])
    y_b = _dot(mx_ref[...], wfnet_ref[...])
    merged = (jax.nn.sigmoid(za_ref[...].astype(F32)) * y_a
              + jax.nn.sigmoid(zb_ref[...].astype(F32)) * y_b)
    x = x + _dot(merged.astype(BF16), wout_ref[...])
    h = _rms(x, gffn_ref[...]).astype(BF16)
    up = jnp.maximum(_dot(h, wup_ref[...]), 0.0)
    x = x + _dot((up * up).astype(BF16), wdown_ref[...])
    h = _rms(x, gple_ref[...]).astype(BF16)
    gate = jax.nn.sigmoid(_dot(h, wgate_ref[...]))
    x = x + gate * _dot(p_ref[...].astype(BF16), wple_ref[...])
    o_ref[...] = _rms(x, gfin_ref[...]) if final_norm else x


def _stage_d(x2, u, y, mx, za, zb, p2, consts, final_norm):
    n = x2.shape[0]
    row = lambda w: pl.BlockSpec((TM_D, w), lambda i: (i, 0))
    const = lambda a: pl.BlockSpec(a.shape, lambda i: (0, 0), pipeline_mode=pl.Buffered(1))
    return pl.pallas_call(
        functools.partial(_stage_d_kernel, final_norm=final_norm),
        grid=(n // TM_D,),
        in_specs=[row(D_MODEL), row(SSM_WIDTH), row(SSM_WIDTH), row(FFT_WIDTH),
                  row(D_MODEL), row(D_MODEL), row(D_PLE)] + [const(c) for c in consts],
        out_specs=row(D_MODEL),
        out_shape=jax.ShapeDtypeStruct((n, D_MODEL), F32),
        compiler_params=pltpu.CompilerParams(
            dimension_semantics=("parallel",), vmem_limit_bytes=VMEM_LIMIT),
        name="stage_d_merge_mlp",
    )(x2, u, y, mx, za, zb, p2, *consts)


def kernel(x, p, g_mix, w_in, lam_re, lam_im, log_dt, b_re, b_im, c_re, c_im, d_skip,
           w_glu, w_fnet, w_out, g_ffn, w_up, w_down, g_ple, w_ple_gate, w_ple, g_final):
    depth = w_in.shape[0]
    bsz, seq, d = x.shape
    assert d == D_MODEL and seq % CHUNK == 0 and (bsz * seq) % TM_A == 0
    assert bsz == 2 * GROUPS_PER_TILE
    n_chunks = seq // CHUNK
    n = bsz * seq
    assert seq % TM_A == 0 and TM_A % (2 * GROUPS_PER_TILE * SEQ_RADIX) == 0
    w_ch, w_seq, twc, tws = (jnp.asarray(t) for t in _dft_tables(seq))
    w_ch, w_seq = w_ch.astype(BF16), w_seq.astype(BF16)
    vec = lambda v: v.reshape(1, -1).astype(F32)

    x2 = x.reshape(n, d)
    for i in range(depth):
        u, pf, za, zb = _stage_a(x2, vec(g_mix[i]), w_in[i].astype(BF16), w_ch, bsz, seq)

        u_sb = u.reshape(bsz, seq, SSM_WIDTH).transpose(1, 0, 2).reshape(n, SSM_WIDTH)
        y_sb = _s5_call(u_sb, _s5_params(lam_re[i], lam_im[i], log_dt[i], b_re[i], b_im[i],
                                         c_re[i], c_im[i]), n_chunks, bsz)
        y = y_sb.reshape(seq, bsz, SSM_WIDTH).transpose(1, 0, 2).reshape(n, SSM_WIDTH)

        mx = _fnet_call(pf, w_seq, twc, tws).reshape(n, FFT_WIDTH)

        consts = [vec(d_skip[i]), w_glu[i].astype(BF16), w_fnet[i].astype(BF16),
                  w_out[i].astype(BF16), vec(g_ffn[i]), w_up[i].astype(BF16),
                  w_down[i].astype(BF16), vec(g_ple[i]), w_ple_gate[i].astype(BF16),
                  w_ple[i].astype(BF16), vec(g_final)]
        x2 = _stage_d(x2, u, y, mx, za, zb, p[i].reshape(n, D_PLE), consts, i == depth - 1)
    return x2.reshape(bsz, seq, d)
```

```python
import functools
import math

import jax
import jax.numpy as jnp
import numpy as np
from jax import lax
from jax.experimental import pallas as pl
from jax.experimental.pallas import tpu as pltpu

D_MODEL = 1024
SSM_WIDTH = 512
SSM_GROUP = 16
SSM_GROUPS = 32
SSM_STATE = 64
FFT_WIDTH = 512
FFT_GROUPS = 4
FFT_GROUP = 128
D_FF = 4096
D_PLE = 256
EPS = 1e-6

CHUNK = 16
CHUNK_VEC = CHUNK * SSM_GROUP
SEQ_RADIX = 8
LANES = 128

TM_A = 1024
TM_D = 512
VMEM_LIMIT = 56 * 1024 * 1024
S5_VMEM_LIMIT = 60 * 1024 * 1024

F32 = jnp.float32
BF16 = jnp.bfloat16
U32 = jnp.uint32


def _rms(x, g):
    ms = jnp.mean(x * x, axis=-1, keepdims=True)
    return (x * lax.rsqrt(ms + EPS)) * g


def _dot(a, b):
    return jnp.dot(a, b, preferred_element_type=F32)


def _dot_nt(a, b, precision=None):
    return lax.dot_general(a, b, (((1,), (1,)), ((), ())), precision=precision,
                           preferred_element_type=F32)


def _stage_a_kernel(x_ref, g_ref, win_ref, wdft_ref, *rest, n_cast):
    cast_in, rest = rest[:n_cast], rest[n_cast:]
    u_ref, pf_ref, za_ref, zb_ref = rest[:4]
    cast_out, z_ref = rest[4:4 + n_cast], rest[4 + n_cast]
    for src, dst in zip(cast_in, cast_out):
        dst[...] = src[...].astype(BF16)
    h = _rms(x_ref[...], g_ref[...]).astype(BF16)
    proj = _dot(h, win_ref[...])
    u_ref[...] = proj[:, :SSM_WIDTH].astype(BF16)
    uf = proj[:, SSM_WIDTH:SSM_WIDTH + FFT_WIDTH].astype(BF16)
    for gi in range(FFT_GROUPS):
        z = _dot(uf[:, gi * FFT_GROUP:(gi + 1) * FFT_GROUP], wdft_ref[...])
        z_ref[gi] = z[:, :FFT_GROUP]
        z_ref[FFT_GROUPS + gi] = z[:, FFT_GROUP:]
    for s2 in range(SEQ_RADIX):
        for part in range(2):
            for gi in range(FFT_GROUPS):
                zs = z_ref[part * FFT_GROUPS + gi, pl.ds(s2, TM_A // SEQ_RADIX, stride=SEQ_RADIX), :]
                pf_ref[0, s2, part, :, gi * FFT_GROUP:(gi + 1) * FFT_GROUP] = zs.astype(BF16)
    za_ref[...] = proj[:, 2 * 512:2 * 512 + D_MODEL].astype(BF16)
    zb_ref[...] = proj[:, 2 * 512 + D_MODEL:].astype(BF16)


def _stage_a(x2, g_mix, w_in_bf, w_dft, to_cast, bsz, seq):
    n = x2.shape[0]
    steps = n // TM_A
    tiles_per_seq = seq // TM_A
    seq_sub = seq // SEQ_RADIX
    row = lambda w: pl.BlockSpec((TM_A, w), lambda i: (i, 0))
    full = lambda a: pl.BlockSpec(a.shape, lambda i: (0,) * a.ndim)
    pf_spec = pl.BlockSpec((1, SEQ_RADIX, 2, TM_A // SEQ_RADIX, FFT_WIDTH),
                           lambda i: (i // tiles_per_seq, 0, 0, i % tiles_per_seq, 0))
    for w in to_cast:
        assert w.shape[0] % (steps * 2 * GROUPS_PER_TILE) == 0
    cast_specs = [pl.BlockSpec((w.shape[0] // steps, w.shape[1]), lambda i: (i, 0)) for w in to_cast]
    return pl.pallas_call(
        functools.partial(_stage_a_kernel, n_cast=len(to_cast)),
        grid=(steps,),
        in_specs=[row(D_MODEL), full(g_mix),
                  pl.BlockSpec(w_in_bf.shape, lambda i: (0, 0), pipeline_mode=pl.Buffered(1)),
                  full(w_dft)] + cast_specs,
        out_specs=[row(SSM_WIDTH), pf_spec, row(D_MODEL), row(D_MODEL)] + cast_specs,
        out_shape=[jax.ShapeDtypeStruct((n, SSM_WIDTH), BF16),
                   jax.ShapeDtypeStruct((bsz, SEQ_RADIX, 2, seq_sub, FFT_WIDTH), BF16),
                   jax.ShapeDtypeStruct((n, D_MODEL), BF16),
                   jax.ShapeDtypeStruct((n, D_MODEL), BF16)]
                  + [jax.ShapeDtypeStruct(w.shape, BF16) for w in to_cast],
        scratch_shapes=[pltpu.VMEM((2 * FFT_GROUPS, TM_A, FFT_GROUP), F32)],
        compiler_params=pltpu.CompilerParams(
            dimension_semantics=("parallel",), vmem_limit_bytes=VMEM_LIMIT),
        name="stage_a_inproj",
    )(x2, g_mix, w_in_bf, w_dft, *to_cast)


GROUPS_PER_TILE = LANES // SSM_GROUP
SCAN_GROUPS = 4
RELAYOUT_UNROLL = 4


def _chunk_time_order(slot_shift):
    gpt = GROUPS_PER_TILE
    return [gpt * (pos // gpt) + (pos % gpt - slot_shift) % gpt for pos in range(CHUNK)]


def _group_operators(lam_re, lam_im, log_dt, btr, bti, cr, ci, slot_shift):
    order = _chunk_time_order(slot_shift)
    half = SSM_STATE
    is_f = lax.broadcasted_iota(jnp.int32, (1, 2 * half), 1) < half
    dt = jnp.exp(log_dt)
    lr, li = lam_re * dt, lam_im * dt
    mag = jnp.exp(lr)
    a1r, a1i = mag * jnp.cos(li), mag * jnp.sin(li)
    den = lam_re * lam_re + lam_im * lam_im
    kr = ((a1r - 1.0) * lam_re + a1i * lam_im) / den
    ki = (a1i * lam_re - (a1r - 1.0) * lam_im) / den
    bbr, bbi = kr * btr - ki * bti, kr * bti + ki * btr
    pw = [(jnp.ones_like(a1r), jnp.zeros_like(a1r))]
    for _ in range(CHUNK):
        pr, pi = pw[-1]
        pw.append((pr * a1r - pi * a1i, pr * a1i + pi * a1r))

    def dir_pick(seq, tf, tb):
        return jnp.where(is_f, seq[tf], seq[tb])

    pwr, pwi = [q[0] for q in pw], [q[1] for q in pw]
    m_rows = []
    for t in range(CHUNK):
        ar, ai = dir_pick(pwr, CHUNK - 1 - t, t), dir_pick(pwi, CHUNK - 1 - t, t)
        m_rows.append(jnp.concatenate([bbr * ar - bbi * ai, bbr * ai + bbi * ar], axis=1))
    m_mat = jnp.concatenate([m_rows[t] for t in order], axis=0)

    car = [cr * pwr[k] - ci * pwi[k] for k in range(CHUNK + 1)]
    cai = [cr * pwi[k] + ci * pwr[k] for k in range(CHUNK + 1)]
    n_rows = [jnp.concatenate([dir_pick(car, t + 1, CHUNK - t), -dir_pick(cai, t + 1, CHUNK - t)], axis=1)
              for t in range(CHUNK)]
    nt_mat = jnp.concatenate([n_rows[t] for t in order], axis=0)

    k_rows = []
    for i in range(2 * CHUNK - 1):
        lag = abs(i - (CHUNK - 1))
        if i == CHUNK - 1:
            rr, ii = car[lag], cai[lag]
        else:
            keep = is_f if i > CHUNK - 1 else jnp.logical_not(is_f)
            rr, ii = jnp.where(keep, car[lag], 0.0), jnp.where(keep, cai[lag], 0.0)
        k_rows.append(jnp.concatenate([rr, ii], axis=1))
    k_rows.append(jnp.zeros_like(k_rows[0]))
    ca_seq = jnp.concatenate(k_rows, axis=0)
    lb = jnp.concatenate([bbr, -bbi], axis=1)
    k_seq = _dot_nt(lb, ca_seq, precision=lax.Precision.HIGHEST)
    t_rows = []
    for t in range(CHUNK):
        off = SSM_GROUP * (CHUNK - 1 - t)
        t_rows.append(k_seq[:, off:off + CHUNK_VEC])
    t_mat = jnp.concatenate([t_rows[t] for t in order], axis=0)
    if slot_shift:
        t_mat = jnp.concatenate(
            [pltpu.roll(t_mat[:, k * LANES:(k + 1) * LANES], slot_shift * SSM_GROUP, 1)
             for k in range(CHUNK_VEC // LANES)], axis=1)
    return t_mat.astype(BF16), m_mat.astype(BF16), nt_mat.astype(BF16), pwr[CHUNK], pwi[CHUNK]


def _s5_kernel(u_ref, prm_ref, y_ref,
               ug_ref, e_ref, x_ref, t_ref, nt_ref, *, n_chunks, bsz):
    half = SSM_STATE
    gpt = GROUPS_PER_TILE
    slot = lax.broadcasted_iota(jnp.int32, (bsz, LANES), 1) // SSM_GROUP
    slot_is = [slot == s for s in range(gpt)]

    def rot(v, slots):
        if slots % gpt == 0:
            return v
        return pltpu.roll(v.astype(F32), (slots % gpt) * SSM_GROUP, 1).astype(BF16)

    def gather_chunk(c, carry):
        base = c * CHUNK
        r = [rot(u_ref[pl.ds(pl.multiple_of((base + t) * bsz, bsz), bsz), :], t) for t in range(CHUNK)]
        for tile in range(CHUNK // gpt):
            for g in range(gpt):
                d = r[tile * gpt + (-g) % gpt]
                for s in range(1, gpt):
                    d = jnp.where(slot_is[s], r[tile * gpt + (s - g) % gpt], d)
                ug_ref[g, pl.ds(pl.multiple_of(c * bsz, bsz), bsz), tile * LANES:(tile + 1) * LANES] = d
        return carry

    lax.fori_loop(0, n_chunks, gather_chunk, 0, unroll=RELAYOUT_UNROLL)

    is_fwd = lax.broadcasted_iota(jnp.int32, (bsz, 2 * half), 1) < half
    for first in range(0, gpt, SCAN_GROUPS):
        a16 = []
        for k in range(SCAN_GROUPS):
            g = first + k
            t_mat, m_mat, nt_mat, ar, ai = _group_operators(
                *(prm_ref[g, r0:r0 + nr, :] for r0, nr in _S5_PARAM_ROWS), g)
            t_ref[k] = t_mat
            nt_ref[k] = nt_mat
            a16.append((ar, ai))
            e_ref[k] = _dot(ug_ref[g], m_mat)

        def scan_step(i, carry):
            rf = pl.multiple_of(i * bsz, bsz)
            rb = pl.multiple_of((n_chunks - 1 - i) * bsz, bsz)
            out = []
            for k in range(SCAN_GROUPS):
                xr, xi = carry[k]
                ar, ai = a16[k]
                xrb, xib = xr.astype(BF16), xi.astype(BF16)
                x_ref[k, pl.ds(rf, bsz), 0:half] = xrb[:, 0:half]
                x_ref[k, pl.ds(rb, bsz), half:2 * half] = xrb[:, half:]
                x_ref[k, pl.ds(rf, bsz), 2 * half:3 * half] = xib[:, 0:half]
                x_ref[k, pl.ds(rb, bsz), 3 * half:] = xib[:, half:]
                er = jnp.where(is_fwd, e_ref[k, pl.ds(rf, bsz), 0:2 * half],
                               e_ref[k, pl.ds(rb, bsz), 0:2 * half])
                ei = jnp.where(is_fwd, e_ref[k, pl.ds(rf, bsz), 2 * half:],
                               e_ref[k, pl.ds(rb, bsz), 2 * half:])
                out.append((ar * xr - ai * xi + er, ar * xi + ai * xr + ei))
            return tuple(out)

        zero = jnp.zeros((bsz, 2 * half), F32)
        lax.fori_loop(0, n_chunks, scan_step, tuple((zero, zero) for _ in range(SCAN_GROUPS)))

        for k in range(SCAN_GROUPS):
            g = first + k
            y = _dot(ug_ref[g], t_ref[k]) + _dot_nt(x_ref[k], nt_ref[k])
            ug_ref[g] = y.astype(BF16)

    def scatter_chunk(c, carry):
        rows = pl.ds(pl.multiple_of(c * bsz, bsz), bsz)
        base = c * CHUNK
        for tile in range(CHUNK // gpt):
            d = [ug_ref[g, rows, tile * LANES:(tile + 1) * LANES] for g in range(gpt)]
            for i in range(gpt):
                w = d[(-i) % gpt]
                for s in range(1, gpt):
                    w = jnp.where(slot_is[s], d[(s - i) % gpt], w)
                t = tile * gpt + i
                y_ref[pl.ds(pl.multiple_of((base + t) * bsz, bsz), bsz), :] = rot(w, -i)
        return carry

    lax.fori_loop(0, n_chunks, scatter_chunk, 0, unroll=RELAYOUT_UNROLL)


def _s5_call(u_sb, params, n_chunks, bsz):
    n_tok, width = u_sb.shape
    rows = n_chunks * bsz
    gpt = GROUPS_PER_TILE
    return pl.pallas_call(
        functools.partial(_s5_kernel, n_chunks=n_chunks, bsz=bsz),
        grid=(width // LANES,),
        in_specs=[pl.BlockSpec((n_tok, LANES), lambda j: (0, j)),
                  pl.BlockSpec((gpt,) + params.shape[1:], lambda j: (j, 0, 0))],
        out_specs=pl.BlockSpec((n_tok, LANES), lambda j: (0, j), pipeline_mode=pl.Buffered(1)),
        out_shape=jax.ShapeDtypeStruct((n_tok, width), BF16),
        scratch_shapes=[pltpu.VMEM((gpt, rows, CHUNK_VEC), BF16),
                        pltpu.VMEM((SCAN_GROUPS, rows, 4 * SSM_STATE), F32),
                        pltpu.VMEM((SCAN_GROUPS, rows, 4 * SSM_STATE), BF16),
                        pltpu.VMEM((SCAN_GROUPS, CHUNK_VEC, CHUNK_VEC), BF16),
                        pltpu.VMEM((SCAN_GROUPS, CHUNK_VEC, CHUNK_VEC), BF16)],
        compiler_params=pltpu.CompilerParams(
            dimension_semantics=("parallel",), vmem_limit_bytes=S5_VMEM_LIMIT),
        name="s5_chunked_scan",
    )(u_sb, params)


_S5_PARAM_ROWS = ((0, 1), (1, 1), (2, 1), (8, 16), (24, 16), (40, 16), (56, 16))


def _s5_params(lam_re, lam_im, log_dt, b_re, b_im, c_re, c_im):
    cat = lambda a: jnp.concatenate([a[0], a[1]], axis=-1)
    g = lam_re.shape[1]
    ldt = jnp.broadcast_to(log_dt[:, :, None], (2, g, SSM_STATE))
    head = jnp.stack([cat(lam_re), cat(lam_im), cat(ldt)], axis=1)
    head = jnp.pad(head, ((0, 0), (0, 5), (0, 0)))
    return jnp.concatenate([head, cat(b_re.transpose(0, 1, 3, 2)), cat(b_im.transpose(0, 1, 3, 2)),
                            cat(c_re), cat(c_im)], axis=1)


def _dft8_real(a):
    def dft4(b):
        (a0r, a0i), (a1r, a1i), (a2r, a2i), (a3r, a3i) = b
        t0r, t0i, t1r, t1i = a0r + a2r, a0i + a2i, a0r - a2r, a0i - a2i
        t2r, t2i, t3r, t3i = a1r + a3r, a1i + a3i, a1r - a3r, a1i - a3i
        return [(t0r + t2r, t0i + t2i), (t1r + t3i, t1i - t3r), (t0r - t2r, t0i - t2i), (t1r - t3i, t1i + t3r)]

    ev, od = dft4(a[0::2]), dft4(a[1::2])
    rt = math.sqrt(0.5)
    p = [od[0][0], rt * (od[1][0] + od[1][1]), od[2][1], rt * (od[3][1] - od[3][0])]
    return [ev[k][0] + p[k] for k in range(4)] + [ev[k][0] - p[k] for k in range(4)]


def _fnet_kernel(p_ref, w1_ref, twc_ref, tws_ref, o_ref):
    sub = w1_ref.shape[0] // 2
    width = 2 * LANES
    for j in range(FFT_WIDTH // width):
        cols = slice(j * width, (j + 1) * width)
        a = []
        for s2 in range(SEQ_RADIX):
            y = _dot(w1_ref[...], p_ref[0, s2, :, :, cols].reshape(2 * sub, width))
            yr, yi = y[:sub], y[sub:]
            if s2:
                c = jnp.concatenate([twc_ref[s2]] * (width // LANES), axis=1)
                s = jnp.concatenate([tws_ref[s2]] * (width // LANES), axis=1)
                yr, yi = c * yr + s * yi, c * yi - s * yr
            a.append((yr, yi))
        out = _dft8_real(a)
        for m2 in range(SEQ_RADIX):
            o_ref[0, m2 * sub:(m2 + 1) * sub, cols] = out[m2].astype(BF16)


def _fnet_call(pf5, w1, twc, tws):
    b, radix, _, sub, width = pf5.shape
    const = lambda a: pl.BlockSpec(a.shape, lambda i: (0,) * a.ndim)
    return pl.pallas_call(
        _fnet_kernel,
        grid=(b,),
        in_specs=[pl.BlockSpec((1, radix, 2, sub, width), lambda i: (i, 0, 0, 0, 0)),
                  const(w1), const(twc), const(tws)],
        out_specs=pl.BlockSpec((1, radix * sub, width), lambda i: (i, 0, 0)),
        out_shape=jax.ShapeDtypeStruct((b, radix * sub, width), BF16),
        compiler_params=pltpu.CompilerParams(
            dimension_semantics=("parallel",), vmem_limit_bytes=VMEM_LIMIT),
        name="fnet_seq_dft",
    )(pf5, w1, twc, tws)


@functools.lru_cache(maxsize=None)
def _dft_tables(seq):
    c = np.arange(FFT_GROUP)
    ang = 2.0 * np.pi * np.outer(c, c) / FFT_GROUP
    w_ch = np.concatenate([np.cos(ang), -np.sin(ang)], axis=1) / math.sqrt(FFT_GROUP)
    sub = seq // SEQ_RADIX
    m = np.arange(sub)
    ang = 2.0 * np.pi * (np.outer(m, m) % sub) / sub
    cs, sn = np.cos(ang) / math.sqrt(seq), np.sin(ang) / math.sqrt(seq)
    w1 = np.block([[cs, sn], [-sn, cs]])
    ang = 2.0 * np.pi * np.outer(np.arange(SEQ_RADIX), m) / seq
    rep = lambda t: np.repeat(t[:, :, None], LANES, axis=2).astype(np.float32)
    return w_ch.astype(np.float32), w1.astype(np.float32), rep(np.cos(ang)), rep(np.sin(ang))


def _stage_d_kernel(x_ref, u_ref, y_ref, mx_ref, za_ref, zb_ref, p_ref,
                    dskip_ref, wglu_ref, wfnet_ref, wout_ref, gffn_ref, wup_ref, wdown_ref,
                    gple_ref, wgate_ref, wple_ref, gfin_ref, o_ref, *, final_norm):
    x = x_ref[...]
    s = y_ref[...].astype(F32) + dskip_ref[...] * u_ref[...].astype(F32)
    a = 0.5 * s * (1.0 + jnp.tanh(math.sqrt(2.0 / math.pi) * (s + 0.044715 * (s * s * s))))
    vg = _dot(a.astype(BF16), wglu_ref[...])
    y_a = vg[:, :D_MODEL] * jax.nn.sigmoid(vg[:, D_MODEL:])
    y_b = _dot(mx_ref[...], wfnet_ref[...])
    merged = (jax.nn.sigmoid(za_ref[...].astype(F32)) * y_a
              + jax.nn.sigmoid(zb_ref[...].astype(F32)) * y_b)
    x = x + _dot(merged.astype(BF16), wout_ref[...])
    h = _rms(x, gffn_ref[...]).astype(BF16)
    up = jnp.maximum(_dot(h, wup_ref[...]), 0.0)
    x = x + _dot((up * up).astype(BF16), wdown_ref[...])
    h = _rms(x, gple_ref[...]).astype(BF16)
    gate = jax.nn.sigmoid(_dot(h, wgate_ref[...]))
    x = x + gate * _dot(p_ref[...].astype(BF16), wple_ref[...])
    o_ref[...] = _rms(x, gfin_ref[...]) if final_norm else x


def _stage_d(x2, u, y, mx, za, zb, p2, consts, final_norm):
    n = x2.shape[0]
    row = lambda w: pl.BlockSpec((TM_D, w), lambda i: (i, 0))
    const = lambda a: pl.BlockSpec(a.shape, lambda i: (0, 0), pipeline_mode=pl.Buffered(1))
    return pl.pallas_call(
        functools.partial(_stage_d_kernel, final_norm=final_norm),
        grid=(n // TM_D,),
        in_specs=[row(D_MODEL), row(SSM_WIDTH), row(SSM_WIDTH), row(FFT_WIDTH),
                  row(D_MODEL), row(D_MODEL), row(D_PLE)] + [const(c) for c in consts],
        out_specs=row(D_MODEL),
        out_shape=jax.ShapeDtypeStruct((n, D_MODEL), F32),
        compiler_params=pltpu.CompilerParams(
            dimension_semantics=("parallel",), vmem_limit_bytes=VMEM_LIMIT),
        name="stage_d_merge_mlp",
    )(x2, u, y, mx, za, zb, p2, *consts)


def kernel(x, p, g_mix, w_in, lam_re, lam_im, log_dt, b_re, b_im, c_re, c_im, d_skip,
           w_glu, w_fnet, w_out, g_ffn, w_up, w_down, g_ple, w_ple_gate, w_ple, g_final):
    depth = w_in.shape[0]
    bsz, seq, d = x.shape
    assert d == D_MODEL and seq % CHUNK == 0 and (bsz * seq) % TM_A == 0
    assert bsz == 2 * GROUPS_PER_TILE
    n_chunks = seq // CHUNK
    n = bsz * seq
    assert seq % TM_A == 0 and TM_A % (2 * GROUPS_PER_TILE * SEQ_RADIX) == 0
    w_ch, w_seq, twc, tws = (jnp.asarray(t) for t in _dft_tables(seq))
    w_ch, w_seq = w_ch.astype(BF16), w_seq.astype(BF16)
    vec = lambda v: v.reshape(1, -1).astype(F32)

    x2 = x.reshape(n, d)
    for i in range(depth):
        u, pf, za, zb, glu_b, fnet_b, out_b, up_b, down_b, gate_b = _stage_a(
            x2, vec(g_mix[i]), w_in[i].astype(BF16), w_ch,
            [w_glu[i], w_fnet[i], w_out[i], w_up[i], w_down[i], w_ple_gate[i]], bsz, seq)

        u_sb = u.reshape(bsz, seq, SSM_WIDTH).transpose(1, 0, 2).reshape(n, SSM_WIDTH)
        y_sb = _s5_call(u_sb, _s5_params(lam_re[i], lam_im[i], log_dt[i], b_re[i], b_im[i],
                                         c_re[i], c_im[i]), n_chunks, bsz)
        y = y_sb.reshape(seq, bsz, SSM_WIDTH).transpose(1, 0, 2).reshape(n, SSM_WIDTH)

        mx = _fnet_call(pf, w_seq, twc, tws).reshape(n, FFT_WIDTH)

        consts = [vec(d_skip[i]), glu_b, fnet_b, out_b, vec(g_ffn[i]), up_b, down_b,
                  vec(g_ple[i]), gate_b, w_ple[i].astype(BF16), vec(g_final)]
        x2 = _stage_d(x2, u, y, mx, za, zb, p[i].reshape(n, D_PLE), consts, i == depth - 1)
    return x2.reshape(bsz, seq, d)
```

```python
import functools
import math

import jax
import jax.numpy as jnp
import numpy as np
from jax import lax
from jax.experimental import pallas as pl
from jax.experimental.pallas import tpu as pltpu

D_MODEL = 1024
SSM_WIDTH = 512
SSM_GROUP = 16
SSM_GROUPS = 32
SSM_STATE = 64
FFT_WIDTH = 512
FFT_GROUPS = 4
FFT_GROUP = 128
D_FF = 4096
D_PLE = 256
EPS = 1e-6

CHUNK = 16
CHUNK_VEC = CHUNK * SSM_GROUP
SEQ_RADIX = 8
LANES = 128

TM_A = 1024
TM_D = 512
VMEM_LIMIT = 56 * 1024 * 1024
S5_VMEM_LIMIT = 60 * 1024 * 1024

F32 = jnp.float32
BF16 = jnp.bfloat16
U32 = jnp.uint32


def _rms(x, g):
    ms = jnp.mean(x * x, axis=-1, keepdims=True)
    return (x * lax.rsqrt(ms + EPS)) * g


def _dot(a, b):
    return jnp.dot(a, b, preferred_element_type=F32)


def _dot_nt(a, b, precision=None):
    return lax.dot_general(a, b, (((1,), (1,)), ((), ())), precision=precision,
                           preferred_element_type=F32)


def _stage_a_kernel(x_ref, g_ref, win_ref, wdft_ref, *rest, n_cast):
    cast_in, rest = rest[:n_cast], rest[n_cast:]
    u_ref, pf_ref, za_ref, zb_ref = rest[:4]
    cast_out, z_ref = rest[4:4 + n_cast], rest[4 + n_cast]
    for src, dst in zip(cast_in, cast_out):
        dst[...] = src[...].astype(BF16)
    h = _rms(x_ref[...], g_ref[...]).astype(BF16)
    uf = _dot(h, win_ref[:, SSM_WIDTH:SSM_WIDTH + FFT_WIDTH]).astype(BF16)
    for gi in range(FFT_GROUPS):
        z = _dot(uf[:, gi * FFT_GROUP:(gi + 1) * FFT_GROUP], wdft_ref[...])
        z_ref[gi] = z[:, :FFT_GROUP]
        z_ref[FFT_GROUPS + gi] = z[:, FFT_GROUP:]
    for s2 in range(SEQ_RADIX):
        for part in range(2):
            for gi in range(FFT_GROUPS):
                zs = z_ref[part * FFT_GROUPS + gi, pl.ds(s2, TM_A // SEQ_RADIX, stride=SEQ_RADIX), :]
                pf_ref[0, s2, part, :, gi * FFT_GROUP:(gi + 1) * FFT_GROUP] = zs.astype(BF16)
    gates = SSM_WIDTH + FFT_WIDTH
    u_ref[...] = _dot(h, win_ref[:, :SSM_WIDTH]).astype(BF16)
    za_ref[...] = _dot(h, win_ref[:, gates:gates + D_MODEL]).astype(BF16)
    zb_ref[...] = _dot(h, win_ref[:, gates + D_MODEL:]).astype(BF16)


def _stage_a(x2, g_mix, w_in_bf, w_dft, to_cast, bsz, seq):
    n = x2.shape[0]
    steps = n // TM_A
    tiles_per_seq = seq // TM_A
    seq_sub = seq // SEQ_RADIX
    row = lambda w: pl.BlockSpec((TM_A, w), lambda i: (i, 0))
    full = lambda a: pl.BlockSpec(a.shape, lambda i: (0,) * a.ndim)
    pf_spec = pl.BlockSpec((1, SEQ_RADIX, 2, TM_A // SEQ_RADIX, FFT_WIDTH),
                           lambda i: (i // tiles_per_seq, 0, 0, i % tiles_per_seq, 0))
    for w in to_cast:
        assert w.shape[0] % (steps * 2 * GROUPS_PER_TILE) == 0
    cast_specs = [pl.BlockSpec((w.shape[0] // steps, w.shape[1]), lambda i: (i, 0)) for w in to_cast]
    return pl.pallas_call(
        functools.partial(_stage_a_kernel, n_cast=len(to_cast)),
        grid=(steps,),
        in_specs=[row(D_MODEL), full(g_mix),
                  pl.BlockSpec(w_in_bf.shape, lambda i: (0, 0), pipeline_mode=pl.Buffered(1)),
                  full(w_dft)] + cast_specs,
        out_specs=[row(SSM_WIDTH), pf_spec, row(D_MODEL), row(D_MODEL)] + cast_specs,
        out_shape=[jax.ShapeDtypeStruct((n, SSM_WIDTH), BF16),
                   jax.ShapeDtypeStruct((bsz, SEQ_RADIX, 2, seq_sub, FFT_WIDTH), BF16),
                   jax.ShapeDtypeStruct((n, D_MODEL), BF16),
                   jax.ShapeDtypeStruct((n, D_MODEL), BF16)]
                  + [jax.ShapeDtypeStruct(w.shape, BF16) for w in to_cast],
        scratch_shapes=[pltpu.VMEM((2 * FFT_GROUPS, TM_A, FFT_GROUP), F32)],
        compiler_params=pltpu.CompilerParams(
            dimension_semantics=("parallel",), vmem_limit_bytes=VMEM_LIMIT),
        name="stage_a_inproj",
    )(x2, g_mix, w_in_bf, w_dft, *to_cast)


GROUPS_PER_TILE = LANES // SSM_GROUP
SCAN_GROUPS = 4
RELAYOUT_UNROLL = 4


def _chunk_time_order(slot_shift):
    gpt = GROUPS_PER_TILE
    return [gpt * (pos // gpt) + (pos % gpt - slot_shift) % gpt for pos in range(CHUNK)]


def _group_operators(lam_re, lam_im, log_dt, btr, bti, cr, ci, slot_shift):
    order = _chunk_time_order(slot_shift)
    half = SSM_STATE
    is_f = lax.broadcasted_iota(jnp.int32, (1, 2 * half), 1) < half
    dt = jnp.exp(log_dt)
    lr, li = lam_re * dt, lam_im * dt
    mag = jnp.exp(lr)
    a1r, a1i = mag * jnp.cos(li), mag * jnp.sin(li)
    den = lam_re * lam_re + lam_im * lam_im
    kr = ((a1r - 1.0) * lam_re + a1i * lam_im) / den
    ki = (a1i * lam_re - (a1r - 1.0) * lam_im) / den
    bbr, bbi = kr * btr - ki * bti, kr * bti + ki * btr
    pw = [(jnp.ones_like(a1r), jnp.zeros_like(a1r))]
    for _ in range(CHUNK):
        pr, pi = pw[-1]
        pw.append((pr * a1r - pi * a1i, pr * a1i + pi * a1r))

    def dir_pick(seq, tf, tb):
        return jnp.where(is_f, seq[tf], seq[tb])

    pwr, pwi = [q[0] for q in pw], [q[1] for q in pw]
    m_rows = []
    for t in range(CHUNK):
        ar, ai = dir_pick(pwr, CHUNK - 1 - t, t), dir_pick(pwi, CHUNK - 1 - t, t)
        m_rows.append(jnp.concatenate([bbr * ar - bbi * ai, bbr * ai + bbi * ar], axis=1))
    m_mat = jnp.concatenate([m_rows[t] for t in order], axis=0)

    car = [cr * pwr[k] - ci * pwi[k] for k in range(CHUNK + 1)]
    cai = [cr * pwi[k] + ci * pwr[k] for k in range(CHUNK + 1)]
    n_rows = [jnp.concatenate([dir_pick(car, t + 1, CHUNK - t), -dir_pick(cai, t + 1, CHUNK - t)], axis=1)
              for t in range(CHUNK)]
    nt_mat = jnp.concatenate([n_rows[t] for t in order], axis=0)

    k_rows = []
    for i in range(2 * CHUNK - 1):
        lag = abs(i - (CHUNK - 1))
        if i == CHUNK - 1:
            rr, ii = car[lag], cai[lag]
        else:
            keep = is_f if i > CHUNK - 1 else jnp.logical_not(is_f)
            rr, ii = jnp.where(keep, car[lag], 0.0), jnp.where(keep, cai[lag], 0.0)
        k_rows.append(jnp.concatenate([rr, ii], axis=1))
    k_rows.append(jnp.zeros_like(k_rows[0]))
    ca_seq = jnp.concatenate(k_rows, axis=0)
    lb = jnp.concatenate([bbr, -bbi], axis=1)
    k_seq = _dot_nt(lb, ca_seq, precision=lax.Precision.HIGHEST)
    t_rows = []
    for t in range(CHUNK):
        off = SSM_GROUP * (CHUNK - 1 - t)
        t_rows.append(k_seq[:, off:off + CHUNK_VEC])
    t_mat = jnp.concatenate([t_rows[t] for t in order], axis=0)
    if slot_shift:
        t_mat = jnp.concatenate(
            [pltpu.roll(t_mat[:, k * LANES:(k + 1) * LANES], slot_shift * SSM_GROUP, 1)
             for k in range(CHUNK_VEC // LANES)], axis=1)
    return t_mat.astype(BF16), m_mat.astype(BF16), nt_mat.astype(BF16), pwr[CHUNK], pwi[CHUNK]


def _s5_kernel(u_ref, prm_ref, after_ref, y_ref,
               ug_ref, e_ref, x_ref, t_ref, nt_ref, *, n_chunks, bsz):
    del after_ref
    half = SSM_STATE
    gpt = GROUPS_PER_TILE
    slot = lax.broadcasted_iota(jnp.int32, (bsz, LANES), 1) // SSM_GROUP
    slot_is = [slot == s for s in range(gpt)]

    def rot(v, slots):
        if slots % gpt == 0:
            return v
        return pltpu.roll(v.astype(F32), (slots % gpt) * SSM_GROUP, 1).astype(BF16)

    def gather_chunk(c, carry):
        base = c * CHUNK
        r = [rot(u_ref[pl.ds(pl.multiple_of((base + t) * bsz, bsz), bsz), :], t) for t in range(CHUNK)]
        for tile in range(CHUNK // gpt):
            for g in range(gpt):
                d = r[tile * gpt + (-g) % gpt]
                for s in range(1, gpt):
                    d = jnp.where(slot_is[s], r[tile * gpt + (s - g) % gpt], d)
                ug_ref[g, pl.ds(pl.multiple_of(c * bsz, bsz), bsz), tile * LANES:(tile + 1) * LANES] = d
        return carry

    lax.fori_loop(0, n_chunks, gather_chunk, 0, unroll=RELAYOUT_UNROLL)

    is_fwd = lax.broadcasted_iota(jnp.int32, (bsz, 2 * half), 1) < half
    for first in range(0, gpt, SCAN_GROUPS):
        a16 = []
        for k in range(SCAN_GROUPS):
            g = first + k
            t_mat, m_mat, nt_mat, ar, ai = _group_operators(
                *(prm_ref[g, r0:r0 + nr, :] for r0, nr in _S5_PARAM_ROWS), g)
            t_ref[k] = t_mat
            nt_ref[k] = nt_mat
            a16.append((ar, ai))
            e_ref[k] = _dot(ug_ref[g], m_mat)

        def scan_step(i, carry):
            rf = pl.multiple_of(i * bsz, bsz)
            rb = pl.multiple_of((n_chunks - 1 - i) * bsz, bsz)
            out = []
            for k in range(SCAN_GROUPS):
                xr, xi = carry[k]
                ar, ai = a16[k]
                xrb, xib = xr.astype(BF16), xi.astype(BF16)
                x_ref[k, pl.ds(rf, bsz), 0:half] = xrb[:, 0:half]
                x_ref[k, pl.ds(rb, bsz), half:2 * half] = xrb[:, half:]
                x_ref[k, pl.ds(rf, bsz), 2 * half:3 * half] = xib[:, 0:half]
                x_ref[k, pl.ds(rb, bsz), 3 * half:] = xib[:, half:]
                er = jnp.where(is_fwd, e_ref[k, pl.ds(rf, bsz), 0:2 * half],
                               e_ref[k, pl.ds(rb, bsz), 0:2 * half])
                ei = jnp.where(is_fwd, e_ref[k, pl.ds(rf, bsz), 2 * half:],
                               e_ref[k, pl.ds(rb, bsz), 2 * half:])
                out.append((ar * xr - ai * xi + er, ar * xi + ai * xr + ei))
            return tuple(out)

        zero = jnp.zeros((bsz, 2 * half), F32)
        lax.fori_loop(0, n_chunks, scan_step, tuple((zero, zero) for _ in range(SCAN_GROUPS)))

        for k in range(SCAN_GROUPS):
            g = first + k
            y = _dot(ug_ref[g], t_ref[k]) + _dot_nt(x_ref[k], nt_ref[k])
            ug_ref[g] = y.astype(BF16)

    def scatter_chunk(c, carry):
        rows = pl.ds(pl.multiple_of(c * bsz, bsz), bsz)
        base = c * CHUNK
        for tile in range(CHUNK // gpt):
            d = [ug_ref[g, rows, tile * LANES:(tile + 1) * LANES] for g in range(gpt)]
            for i in range(gpt):
                w = d[(-i) % gpt]
                for s in range(1, gpt):
                    w = jnp.where(slot_is[s], d[(s - i) % gpt], w)
                t = tile * gpt + i
                y_ref[pl.ds(pl.multiple_of((base + t) * bsz, bsz), bsz), :] = rot(w, -i)
        return carry

    lax.fori_loop(0, n_chunks, scatter_chunk, 0, unroll=RELAYOUT_UNROLL)


def _s5_call(u_sb, params, after, n_chunks, bsz):
    n_tok, width = u_sb.shape
    rows = n_chunks * bsz
    gpt = GROUPS_PER_TILE
    return pl.pallas_call(
        functools.partial(_s5_kernel, n_chunks=n_chunks, bsz=bsz),
        grid=(width // LANES,),
        in_specs=[pl.BlockSpec((n_tok, LANES), lambda j: (0, j)),
                  pl.BlockSpec((gpt,) + params.shape[1:], lambda j: (j, 0, 0)),
                  pl.BlockSpec(memory_space=pl.ANY)],
        out_specs=pl.BlockSpec((n_tok, LANES), lambda j: (0, j)),
        out_shape=jax.ShapeDtypeStruct((n_tok, width), BF16),
        scratch_shapes=[pltpu.VMEM((gpt, rows, CHUNK_VEC), BF16),
                        pltpu.VMEM((SCAN_GROUPS, rows, 4 * SSM_STATE), F32),
                        pltpu.VMEM((SCAN_GROUPS, rows, 4 * SSM_STATE), BF16),
                        pltpu.VMEM((SCAN_GROUPS, CHUNK_VEC, CHUNK_VEC), BF16),
                        pltpu.VMEM((SCAN_GROUPS, CHUNK_VEC, CHUNK_VEC), BF16)],
        compiler_params=pltpu.CompilerParams(
            dimension_semantics=("parallel",), vmem_limit_bytes=S5_VMEM_LIMIT),
        name="s5_chunked_scan",
    )(u_sb, params, after)


_S5_PARAM_ROWS = ((0, 1), (1, 1), (2, 1), (8, 16), (24, 16), (40, 16), (56, 16))


def _s5_params(lam_re, lam_im, log_dt, b_re, b_im, c_re, c_im):
    cat = lambda a: jnp.concatenate([a[0], a[1]], axis=-1)
    g = lam_re.shape[1]
    ldt = jnp.broadcast_to(log_dt[:, :, None], (2, g, SSM_STATE))
    head = jnp.stack([cat(lam_re), cat(lam_im), cat(ldt)], axis=1)
    head = jnp.pad(head, ((0, 0), (0, 5), (0, 0)))
    return jnp.concatenate([head, cat(b_re.transpose(0, 1, 3, 2)), cat(b_im.transpose(0, 1, 3, 2)),
                            cat(c_re), cat(c_im)], axis=1)


def _dft8_real(a):
    def dft4(b):
        (a0r, a0i), (a1r, a1i), (a2r, a2i), (a3r, a3i) = b
        t0r, t0i, t1r, t1i = a0r + a2r, a0i + a2i, a0r - a2r, a0i - a2i
        t2r, t2i, t3r, t3i = a1r + a3r, a1i + a3i, a1r - a3r, a1i - a3i
        return [(t0r + t2r, t0i + t2i), (t1r + t3i, t1i - t3r), (t0r - t2r, t0i - t2i), (t1r - t3i, t1i + t3r)]

    ev, od = dft4(a[0::2]), dft4(a[1::2])
    rt = math.sqrt(0.5)
    p = [od[0][0], rt * (od[1][0] + od[1][1]), od[2][1], rt * (od[3][1] - od[3][0])]
    return [ev[k][0] + p[k] for k in range(4)] + [ev[k][0] - p[k] for k in range(4)]


def _fnet_kernel(p_ref, w1_ref, twc_ref, tws_ref, after_ref, o_ref):
    del after_ref
    sub = w1_ref.shape[0] // 2
    width = 2 * LANES
    for j in range(FFT_WIDTH // width):
        cols = slice(j * width, (j + 1) * width)
        a = []
        for s2 in range(SEQ_RADIX):
            y = _dot(w1_ref[...], p_ref[0, s2, :, :, cols].reshape(2 * sub, width))
            yr, yi = y[:sub], y[sub:]
            if s2:
                c = jnp.concatenate([twc_ref[s2]] * (width // LANES), axis=1)
                s = jnp.concatenate([tws_ref[s2]] * (width // LANES), axis=1)
                yr, yi = c * yr + s * yi, c * yi - s * yr
            a.append((yr, yi))
        out = _dft8_real(a)
        for m2 in range(SEQ_RADIX):
            o_ref[0, m2 * sub:(m2 + 1) * sub, cols] = out[m2].astype(BF16)


def _fnet_call(pf5, w1, twc, tws, first, count, after):
    _, radix, _, sub, width = pf5.shape
    const = lambda a: pl.BlockSpec(a.shape, lambda i: (0,) * a.ndim)
    return pl.pallas_call(
        _fnet_kernel,
        grid=(count,),
        in_specs=[pl.BlockSpec((1, radix, 2, sub, width), lambda i: (i + first, 0, 0, 0, 0)),
                  const(w1), const(twc), const(tws), pl.BlockSpec(memory_space=pl.ANY)],
        out_specs=pl.BlockSpec((1, radix * sub, width), lambda i: (i, 0, 0)),
        out_shape=jax.ShapeDtypeStruct((count, radix * sub, width), BF16),
        compiler_params=pltpu.CompilerParams(
            dimension_semantics=("parallel",), vmem_limit_bytes=VMEM_LIMIT),
        name="fnet_seq_dft",
    )(pf5, w1, twc, tws, after)


@functools.lru_cache(maxsize=None)
def _dft_tables(seq):
    c = np.arange(FFT_GROUP)
    ang = 2.0 * np.pi * np.outer(c, c) / FFT_GROUP
    w_ch = np.concatenate([np.cos(ang), -np.sin(ang)], axis=1) / math.sqrt(FFT_GROUP)
    sub = seq // SEQ_RADIX
    m = np.arange(sub)
    ang = 2.0 * np.pi * (np.outer(m, m) % sub) / sub
    cs, sn = np.cos(ang) / math.sqrt(seq), np.sin(ang) / math.sqrt(seq)
    w1 = np.block([[cs, sn], [-sn, cs]])
    ang = 2.0 * np.pi * np.outer(np.arange(SEQ_RADIX), m) / seq
    rep = lambda t: np.repeat(t[:, :, None], LANES, axis=2).astype(np.float32)
    return w_ch.astype(np.float32), w1.astype(np.float32), rep(np.cos(ang)), rep(np.sin(ang))


def _stage_d_kernel(x_ref, u_ref, y_ref, mx0_ref, mx1_ref, za_ref, zb_ref, p_ref,
                    dskip_ref, wglu_ref, wfnet_ref, wout_ref, gffn_ref, wup_ref, wdown_ref,
                    gple_ref, wgate_ref, wple_ref, gfin_ref, o_ref, *, final_norm, half_tiles):
    mx = jnp.where(pl.program_id(0) < half_tiles, mx0_ref[...], mx1_ref[...])
    x = x_ref[...]
    s = y_ref[...].astype(F32) + dskip_ref[...] * u_ref[...].astype(F32)
    a = 0.5 * s * (1.0 + jnp.tanh(math.sqrt(2.0 / math.pi) * (s + 0.044715 * (s * s * s))))
    vg = _dot(a.astype(BF16), wglu_ref[...])
    y_a = vg[:, :D_MODEL] * jax.nn.sigmoid(vg[:, D_MODEL:])
    y_b = _dot(mx, wfnet_ref[...])
    merged = (jax.nn.sigmoid(za_ref[...].astype(F32)) * y_a
              + jax.nn.sigmoid(zb_ref[...].astype(F32)) * y_b)
    x = x + _dot(merged.astype(BF16), wout_ref[...])
    h = _rms(x, gffn_ref[...]).astype(BF16)
    up = jnp.maximum(_dot(h, wup_ref[...]), 0.0)
    x = x + _dot((up * up).astype(BF16), wdown_ref[...])
    h = _rms(x, gple_ref[...]).astype(BF16)
    gate = jax.nn.sigmoid(_dot(h, wgate_ref[...]))
    x = x + gate * _dot(p_ref[...].astype(BF16), wple_ref[...])
    o_ref[...] = _rms(x, gfin_ref[...]) if final_norm else x


def _stage_d(x2, u, y, mx0, mx1, za, zb, p2, consts, final_norm):
    n = x2.shape[0]
    half = n // TM_D // 2
    row = lambda w: pl.BlockSpec((TM_D, w), lambda i: (i, 0))
    mx0_spec = pl.BlockSpec((TM_D, FFT_WIDTH), lambda i: (jnp.minimum(i, half - 1), 0))
    mx1_spec = pl.BlockSpec((TM_D, FFT_WIDTH), lambda i: (jnp.maximum(i - half, 0), 0))
    const = lambda a: pl.BlockSpec(a.shape, lambda i: (0, 0), pipeline_mode=pl.Buffered(1))
    return pl.pallas_call(
        functools.partial(_stage_d_kernel, final_norm=final_norm, half_tiles=half),
        grid=(n // TM_D,),
        in_specs=[row(D_MODEL), row(SSM_WIDTH), row(SSM_WIDTH), mx0_spec, mx1_spec,
                  row(D_MODEL), row(D_MODEL), row(D_PLE)] + [const(c) for c in consts],
        out_specs=row(D_MODEL),
        out_shape=jax.ShapeDtypeStruct((n, D_MODEL), F32),
        compiler_params=pltpu.CompilerParams(
            dimension_semantics=("parallel",), vmem_limit_bytes=VMEM_LIMIT),
        name="stage_d_merge_mlp",
    )(x2, u, y, mx0, mx1, za, zb, p2, *consts)


def kernel(x, p, g_mix, w_in, lam_re, lam_im, log_dt, b_re, b_im, c_re, c_im, d_skip,
           w_glu, w_fnet, w_out, g_ffn, w_up, w_down, g_ple, w_ple_gate, w_ple, g_final):
    depth = w_in.shape[0]
    bsz, seq, d = x.shape
    assert d == D_MODEL and seq % CHUNK == 0 and (bsz * seq) % TM_A == 0
    assert bsz == 2 * GROUPS_PER_TILE
    n_chunks = seq // CHUNK
    n = bsz * seq
    assert seq % TM_A == 0 and TM_A % (2 * GROUPS_PER_TILE * SEQ_RADIX) == 0
    w_ch, w_seq, twc, tws = (jnp.asarray(t) for t in _dft_tables(seq))
    w_ch, w_seq = w_ch.astype(BF16), w_seq.astype(BF16)
    vec = lambda v: v.reshape(1, -1).astype(F32)

    x2 = x.reshape(n, d)
    for i in range(depth):
        u, pf, za, zb, glu_b, fnet_b, out_b, up_b, down_b, gate_b = _stage_a(
            x2, vec(g_mix[i]), w_in[i].astype(BF16), w_ch,
            [w_glu[i], w_fnet[i], w_out[i], w_up[i], w_down[i], w_ple_gate[i]], bsz, seq)

        hb = bsz // 2
        u_sb = u.reshape(bsz, seq, SSM_WIDTH).transpose(1, 0, 2).reshape(n, SSM_WIDTH)
        mx0 = _fnet_call(pf, w_seq, twc, tws, 0, hb, u)
        y_sb = _s5_call(u_sb, _s5_params(lam_re[i], lam_im[i], log_dt[i], b_re[i], b_im[i],
                                         c_re[i], c_im[i]), mx0, n_chunks, bsz)
        mx1 = _fnet_call(pf, w_seq, twc, tws, hb, bsz - hb, y_sb)
        y = y_sb.reshape(seq, bsz, SSM_WIDTH).transpose(1, 0, 2).reshape(n, SSM_WIDTH)

        consts = [vec(d_skip[i]), glu_b, fnet_b, out_b, vec(g_ffn[i]), up_b, down_b,
                  vec(g_ple[i]), gate_b, w_ple[i].astype(BF16), vec(g_final)]
        x2 = _stage_d(x2, u, y, mx0.reshape(n // 2, FFT_WIDTH), mx1.reshape(n // 2, FFT_WIDTH), za, zb,
                      p[i].reshape(n, D_PLE), consts, i == depth - 1)
    return x2.reshape(bsz, seq, d)
```

```python
import functools
import math

import jax
import jax.numpy as jnp
import numpy as np
from jax import lax
from jax.experimental import pallas as pl
from jax.experimental.pallas import tpu as pltpu

D_MODEL = 1024
SSM_WIDTH = 512
SSM_GROUP = 16
SSM_GROUPS = 32
SSM_STATE = 64
FFT_WIDTH = 512
FFT_GROUPS = 4
FFT_GROUP = 128
D_FF = 4096
D_PLE = 256
EPS = 1e-6

CHUNK = 16
CHUNK_VEC = CHUNK * SSM_GROUP
SEQ_RADIX = 8
LANES = 128

TM_A = 1024
TM_D = 512
FNET_BATCH = 2
VMEM_LIMIT = 56 * 1024 * 1024
S5_VMEM_LIMIT = 60 * 1024 * 1024

F32 = jnp.float32
BF16 = jnp.bfloat16
U32 = jnp.uint32


def _rms(x, g):
    ms = jnp.mean(x * x, axis=-1, keepdims=True)
    return (x * lax.rsqrt(ms + EPS)) * g


def _dot(a, b):
    return jnp.dot(a, b, preferred_element_type=F32)


def _dot_nt(a, b, precision=None):
    return lax.dot_general(a, b, (((1,), (1,)), ((), ())), precision=precision,
                           preferred_element_type=F32)


def _stage_a_kernel(x_ref, g_ref, win_ref, wdft_ref, *rest, n_cast):
    cast_in, rest = rest[:n_cast], rest[n_cast:]
    u_ref, pf_ref, za_ref, zb_ref = rest[:4]
    cast_out, z_ref, w_ref = rest[4:4 + n_cast], rest[4 + n_cast], rest[5 + n_cast]

    @pl.when(pl.program_id(0) == 0)
    def _():
        w_ref[...] = win_ref[...].astype(BF16)

    for src, dst in zip(cast_in, cast_out):
        dst[...] = src[...].astype(BF16)
    h = _rms(x_ref[...], g_ref[...]).astype(BF16)
    uf = _dot(h, w_ref[:, SSM_WIDTH:SSM_WIDTH + FFT_WIDTH]).astype(BF16)
    for gi in range(FFT_GROUPS):
        z = _dot(uf[:, gi * FFT_GROUP:(gi + 1) * FFT_GROUP], wdft_ref[...])
        z_ref[gi] = z[:, :FFT_GROUP]
        z_ref[FFT_GROUPS + gi] = z[:, FFT_GROUP:]
    for s2 in range(SEQ_RADIX):
        for part in range(2):
            for gi in range(FFT_GROUPS):
                zs = z_ref[part * FFT_GROUPS + gi, pl.ds(s2, TM_A // SEQ_RADIX, stride=SEQ_RADIX), :]
                pf_ref[0, s2, part, :, gi * FFT_GROUP:(gi + 1) * FFT_GROUP] = zs.astype(BF16)
    gates = SSM_WIDTH + FFT_WIDTH
    u_ref[...] = _dot(h, w_ref[:, :SSM_WIDTH]).astype(BF16)
    za_ref[...] = _dot(h, w_ref[:, gates:gates + D_MODEL]).astype(BF16)
    zb_ref[...] = _dot(h, w_ref[:, gates + D_MODEL:]).astype(BF16)


def _stage_a(x2, g_mix, w_in, w_dft, to_cast, bsz, seq):
    n = x2.shape[0]
    steps = n // TM_A
    tiles_per_seq = seq // TM_A
    seq_sub = seq // SEQ_RADIX
    row = lambda w: pl.BlockSpec((TM_A, w), lambda i: (i, 0))
    full = lambda a: pl.BlockSpec(a.shape, lambda i: (0,) * a.ndim)
    pf_spec = pl.BlockSpec((1, SEQ_RADIX, 2, TM_A // SEQ_RADIX, FFT_WIDTH),
                           lambda i: (i // tiles_per_seq, 0, 0, i % tiles_per_seq, 0))
    for w in to_cast:
        assert w.shape[0] % (steps * 2 * GROUPS_PER_TILE) == 0
    cast_specs = [pl.BlockSpec((w.shape[0] // steps, w.shape[1]), lambda i: (i, 0)) for w in to_cast]
    return pl.pallas_call(
        functools.partial(_stage_a_kernel, n_cast=len(to_cast)),
        grid=(steps,),
        in_specs=[row(D_MODEL), full(g_mix),
                  pl.BlockSpec(w_in.shape, lambda i: (0, 0), pipeline_mode=pl.Buffered(1)),
                  full(w_dft)] + cast_specs,
        out_specs=[row(SSM_WIDTH), pf_spec, row(D_MODEL), row(D_MODEL)] + cast_specs,
        out_shape=[jax.ShapeDtypeStruct((n, SSM_WIDTH), BF16),
                   jax.ShapeDtypeStruct((bsz, SEQ_RADIX, 2, seq_sub, FFT_WIDTH), BF16),
                   jax.ShapeDtypeStruct((n, D_MODEL), BF16),
                   jax.ShapeDtypeStruct((n, D_MODEL), BF16)]
                  + [jax.ShapeDtypeStruct(w.shape, BF16) for w in to_cast],
        scratch_shapes=[pltpu.VMEM((2 * FFT_GROUPS, TM_A, FFT_GROUP), F32), pltpu.VMEM(w_in.shape, BF16)],
        compiler_params=pltpu.CompilerParams(
            dimension_semantics=("arbitrary",), vmem_limit_bytes=VMEM_LIMIT),
        name="stage_a_inproj",
    )(x2, g_mix, w_in, w_dft, *to_cast)


GROUPS_PER_TILE = LANES // SSM_GROUP
SCAN_GROUPS = 4
RELAYOUT_UNROLL = 4


def _chunk_time_order(slot_shift):
    gpt = GROUPS_PER_TILE
    return [gpt * (pos // gpt) + (pos % gpt - slot_shift) % gpt for pos in range(CHUNK)]


def _group_operators(lam_re, lam_im, log_dt, btr, bti, cr, ci, slot_shift):
    order = _chunk_time_order(slot_shift)
    half = SSM_STATE
    is_f = lax.broadcasted_iota(jnp.int32, (1, 2 * half), 1) < half
    dt = jnp.exp(log_dt)
    lr, li = lam_re * dt, lam_im * dt
    mag = jnp.exp(lr)
    a1r, a1i = mag * jnp.cos(li), mag * jnp.sin(li)
    den = lam_re * lam_re + lam_im * lam_im
    kr = ((a1r - 1.0) * lam_re + a1i * lam_im) / den
    ki = (a1i * lam_re - (a1r - 1.0) * lam_im) / den
    bbr, bbi = kr * btr - ki * bti, kr * bti + ki * btr
    pw = [(jnp.ones_like(a1r), jnp.zeros_like(a1r))]
    for _ in range(CHUNK):
        pr, pi = pw[-1]
        pw.append((pr * a1r - pi * a1i, pr * a1i + pi * a1r))

    def dir_pick(seq, tf, tb):
        return jnp.where(is_f, seq[tf], seq[tb])

    pwr, pwi = [q[0] for q in pw], [q[1] for q in pw]
    m_rows = []
    for t in range(CHUNK):
        ar, ai = dir_pick(pwr, CHUNK - 1 - t, t), dir_pick(pwi, CHUNK - 1 - t, t)
        m_rows.append(jnp.concatenate([bbr * ar - bbi * ai, bbr * ai + bbi * ar], axis=1))
    m_mat = jnp.concatenate([m_rows[t] for t in order], axis=0)

    car = [cr * pwr[k] - ci * pwi[k] for k in range(CHUNK + 1)]
    cai = [cr * pwi[k] + ci * pwr[k] for k in range(CHUNK + 1)]
    n_rows = [jnp.concatenate([dir_pick(car, t + 1, CHUNK - t), -dir_pick(cai, t + 1, CHUNK - t)], axis=1)
              for t in range(CHUNK)]
    nt_mat = jnp.concatenate([n_rows[t] for t in order], axis=0)

    k_rows = []
    for i in range(2 * CHUNK - 1):
        lag = abs(i - (CHUNK - 1))
        if i == CHUNK - 1:
            rr, ii = car[lag], cai[lag]
        else:
            keep = is_f if i > CHUNK - 1 else jnp.logical_not(is_f)
            rr, ii = jnp.where(keep, car[lag], 0.0), jnp.where(keep, cai[lag], 0.0)
        k_rows.append(jnp.concatenate([rr, ii], axis=1))
    k_rows.append(jnp.zeros_like(k_rows[0]))
    ca_seq = jnp.concatenate(k_rows, axis=0)
    lb = jnp.concatenate([bbr, -bbi], axis=1)
    k_seq = _dot_nt(lb, ca_seq, precision=lax.Precision.HIGHEST)
    t_rows = []
    for t in range(CHUNK):
        off = SSM_GROUP * (CHUNK - 1 - t)
        t_rows.append(k_seq[:, off:off + CHUNK_VEC])
    t_mat = jnp.concatenate([t_rows[t] for t in order], axis=0)
    if slot_shift:
        t_mat = jnp.concatenate(
            [pltpu.roll(t_mat[:, k * LANES:(k + 1) * LANES], slot_shift * SSM_GROUP, 1)
             for k in range(CHUNK_VEC // LANES)], axis=1)
    return t_mat.astype(BF16), m_mat.astype(BF16), nt_mat.astype(BF16), pwr[CHUNK], pwi[CHUNK]


def _s5_kernel(u_ref, prm_ref, after_ref, y_ref,
               ug_ref, e_ref, x_ref, t_ref, nt_ref, *, n_chunks, bsz):
    del after_ref
    half = SSM_STATE
    gpt = GROUPS_PER_TILE
    slot = lax.broadcasted_iota(jnp.int32, (bsz, LANES), 1) // SSM_GROUP
    slot_is = [slot == s for s in range(gpt)]

    def rot(v, slots):
        if slots % gpt == 0:
            return v
        return pltpu.roll(v.astype(F32), (slots % gpt) * SSM_GROUP, 1).astype(BF16)

    def gather_chunk(c, carry):
        base = c * CHUNK
        r = [rot(u_ref[pl.ds(pl.multiple_of((base + t) * bsz, bsz), bsz), :], t) for t in range(CHUNK)]
        for tile in range(CHUNK // gpt):
            for g in range(gpt):
                d = r[tile * gpt + (-g) % gpt]
                for s in range(1, gpt):
                    d = jnp.where(slot_is[s], r[tile * gpt + (s - g) % gpt], d)
                ug_ref[g, pl.ds(pl.multiple_of(c * bsz, bsz), bsz), tile * LANES:(tile + 1) * LANES] = d
        return carry

    lax.fori_loop(0, n_chunks, gather_chunk, 0, unroll=RELAYOUT_UNROLL)

    is_fwd = lax.broadcasted_iota(jnp.int32, (bsz, 2 * half), 1) < half
    for first in range(0, gpt, SCAN_GROUPS):
        a16 = []
        for k in range(SCAN_GROUPS):
            g = first + k
            t_mat, m_mat, nt_mat, ar, ai = _group_operators(
                *(prm_ref[g, r0:r0 + nr, :] for r0, nr in _S5_PARAM_ROWS), g)
            t_ref[k] = t_mat
            nt_ref[k] = nt_mat
            a16.append((ar, ai))
            e_ref[k] = _dot(ug_ref[g], m_mat)

        def scan_step(i, carry):
            rf = pl.multiple_of(i * bsz, bsz)
            rb = pl.multiple_of((n_chunks - 1 - i) * bsz, bsz)
            out = []
            for k in range(SCAN_GROUPS):
                xr, xi = carry[k]
                ar, ai = a16[k]
                xrb, xib = xr.astype(BF16), xi.astype(BF16)
                x_ref[k, pl.ds(rf, bsz), 0:half] = xrb[:, 0:half]
                x_ref[k, pl.ds(rb, bsz), half:2 * half] = xrb[:, half:]
                x_ref[k, pl.ds(rf, bsz), 2 * half:3 * half] = xib[:, 0:half]
                x_ref[k, pl.ds(rb, bsz), 3 * half:] = xib[:, half:]
                er = jnp.where(is_fwd, e_ref[k, pl.ds(rf, bsz), 0:2 * half],
                               e_ref[k, pl.ds(rb, bsz), 0:2 * half])
                ei = jnp.where(is_fwd, e_ref[k, pl.ds(rf, bsz), 2 * half:],
                               e_ref[k, pl.ds(rb, bsz), 2 * half:])
                out.append((ar * xr - ai * xi + er, ar * xi + ai * xr + ei))
            return tuple(out)

        zero = jnp.zeros((bsz, 2 * half), F32)
        lax.fori_loop(0, n_chunks, scan_step, tuple((zero, zero) for _ in range(SCAN_GROUPS)))

        for k in range(SCAN_GROUPS):
            g = first + k
            y = _dot(ug_ref[g], t_ref[k]) + _dot_nt(x_ref[k], nt_ref[k])
            ug_ref[g] = y.astype(BF16)

    def scatter_chunk(c, carry):
        rows = pl.ds(pl.multiple_of(c * bsz, bsz), bsz)
        base = c * CHUNK
        for tile in range(CHUNK // gpt):
            d = [ug_ref[g, rows, tile * LANES:(tile + 1) * LANES] for g in range(gpt)]
            for i in range(gpt):
                w = d[(-i) % gpt]
                for s in range(1, gpt):
                    w = jnp.where(slot_is[s], d[(s - i) % gpt], w)
                t = tile * gpt + i
                y_ref[pl.ds(pl.multiple_of((base + t) * bsz, bsz), bsz), :] = rot(w, -i)
        return carry

    lax.fori_loop(0, n_chunks, scatter_chunk, 0, unroll=RELAYOUT_UNROLL)


def _s5_call(u_sb, params, after, n_chunks, bsz):
    n_tok, width = u_sb.shape
    rows = n_chunks * bsz
    gpt = GROUPS_PER_TILE
    return pl.pallas_call(
        functools.partial(_s5_kernel, n_chunks=n_chunks, bsz=bsz),
        grid=(width // LANES,),
        in_specs=[pl.BlockSpec((n_tok, LANES), lambda j: (0, j)),
                  pl.BlockSpec((gpt,) + params.shape[1:], lambda j: (j, 0, 0)),
                  pl.BlockSpec(memory_space=pl.ANY)],
        out_specs=pl.BlockSpec((n_tok, LANES), lambda j: (0, j)),
        out_shape=jax.ShapeDtypeStruct((n_tok, width), BF16),
        scratch_shapes=[pltpu.VMEM((gpt, rows, CHUNK_VEC), BF16),
                        pltpu.VMEM((SCAN_GROUPS, rows, 4 * SSM_STATE), F32),
                        pltpu.VMEM((SCAN_GROUPS, rows, 4 * SSM_STATE), BF16),
                        pltpu.VMEM((SCAN_GROUPS, CHUNK_VEC, CHUNK_VEC), BF16),
                        pltpu.VMEM((SCAN_GROUPS, CHUNK_VEC, CHUNK_VEC), BF16)],
        compiler_params=pltpu.CompilerParams(
            dimension_semantics=("parallel",), vmem_limit_bytes=S5_VMEM_LIMIT),
        name="s5_chunked_scan",
    )(u_sb, params, after)


_S5_PARAM_ROWS = ((0, 1), (1, 1), (2, 1), (8, 16), (24, 16), (40, 16), (56, 16))


def _s5_params(lam_re, lam_im, log_dt, b_re, b_im, c_re, c_im):
    cat = lambda a: jnp.concatenate([a[0], a[1]], axis=-1)
    g = lam_re.shape[1]
    ldt = jnp.broadcast_to(log_dt[:, :, None], (2, g, SSM_STATE))
    head = jnp.stack([cat(lam_re), cat(lam_im), cat(ldt)], axis=1)
    head = jnp.pad(head, ((0, 0), (0, 5), (0, 0)))
    return jnp.concatenate([head, cat(b_re.transpose(0, 1, 3, 2)), cat(b_im.transpose(0, 1, 3, 2)),
                            cat(c_re), cat(c_im)], axis=1)


def _dft8_real(a):
    def dft4(b):
        (a0r, a0i), (a1r, a1i), (a2r, a2i), (a3r, a3i) = b
        t0r, t0i, t1r, t1i = a0r + a2r, a0i + a2i, a0r - a2r, a0i - a2i
        t2r, t2i, t3r, t3i = a1r + a3r, a1i + a3i, a1r - a3r, a1i - a3i
        return [(t0r + t2r, t0i + t2i), (t1r + t3i, t1i - t3r), (t0r - t2r, t0i - t2i), (t1r - t3i, t1i + t3r)]

    ev, od = dft4(a[0::2]), dft4(a[1::2])
    rt = math.sqrt(0.5)
    p = [od[0][0], rt * (od[1][0] + od[1][1]), od[2][1], rt * (od[3][1] - od[3][0])]
    return [ev[k][0] + p[k] for k in range(4)] + [ev[k][0] - p[k] for k in range(4)]


def _fnet_kernel(p_ref, w1_ref, twc_ref, tws_ref, after_ref, o_ref):
    del after_ref
    sub = w1_ref.shape[0] // 2
    width = 2 * LANES
    for b in range(p_ref.shape[0]):
        for j in range(FFT_WIDTH // width):
            cols = slice(j * width, (j + 1) * width)
            a = []
            for s2 in range(SEQ_RADIX):
                y = _dot(w1_ref[...], p_ref[b, s2, :, :, cols].reshape(2 * sub, width))
                yr, yi = y[:sub], y[sub:]
                if s2:
                    c = jnp.concatenate([twc_ref[s2]] * (width // LANES), axis=1)
                    s = jnp.concatenate([tws_ref[s2]] * (width // LANES), axis=1)
                    yr, yi = c * yr + s * yi, c * yi - s * yr
                a.append((yr, yi))
            out = _dft8_real(a)
            for m2 in range(SEQ_RADIX):
                o_ref[b, m2 * sub:(m2 + 1) * sub, cols] = out[m2].astype(BF16)


def _fnet_call(pf5, w1, twc, tws, first, count, after):
    _, radix, _, sub, width = pf5.shape
    assert first % FNET_BATCH == 0 and count % FNET_BATCH == 0
    const = lambda a: pl.BlockSpec(a.shape, lambda i: (0,) * a.ndim)
    return pl.pallas_call(
        _fnet_kernel,
        grid=(count // FNET_BATCH,),
        in_specs=[pl.BlockSpec((FNET_BATCH, radix, 2, sub, width),
                               lambda i: (i + first // FNET_BATCH, 0, 0, 0, 0)),
                  const(w1), const(twc), const(tws), pl.BlockSpec(memory_space=pl.ANY)],
        out_specs=pl.BlockSpec((FNET_BATCH, radix * sub, width), lambda i: (i, 0, 0)),
        out_shape=jax.ShapeDtypeStruct((count, radix * sub, width), BF16),
        compiler_params=pltpu.CompilerParams(
            dimension_semantics=("parallel",), vmem_limit_bytes=VMEM_LIMIT),
        name="fnet_seq_dft",
    )(pf5, w1, twc, tws, after)


@functools.lru_cache(maxsize=None)
def _dft_tables(seq):
    c = np.arange(FFT_GROUP)
    ang = 2.0 * np.pi * np.outer(c, c) / FFT_GROUP
    w_ch = np.concatenate([np.cos(ang), -np.sin(ang)], axis=1) / math.sqrt(FFT_GROUP)
    sub = seq // SEQ_RADIX
    m = np.arange(sub)
    ang = 2.0 * np.pi * (np.outer(m, m) % sub) / sub
    cs, sn = np.cos(ang) / math.sqrt(seq), np.sin(ang) / math.sqrt(seq)
    w1 = np.block([[cs, sn], [-sn, cs]])
    ang = 2.0 * np.pi * np.outer(np.arange(SEQ_RADIX), m) / seq
    rep = lambda t: np.repeat(t[:, :, None], LANES, axis=2).astype(np.float32)
    return w_ch.astype(np.float32), w1.astype(np.float32), rep(np.cos(ang)), rep(np.sin(ang))


def _stage_d_kernel(x_ref, u_ref, y_ref, mx0_ref, mx1_ref, za_ref, zb_ref, p_ref,
                    dskip_ref, wglu_ref, wfnet_ref, wout_ref, gffn_ref, wup_ref, wdown_ref,
                    gple_ref, wgate_ref, wple_ref, gfin_ref, o_ref, *, final_norm, half_tiles):
    mx = jnp.where(pl.program_id(0) < half_tiles, mx0_ref[...], mx1_ref[...])
    x = x_ref[...]
    s = y_ref[...].astype(F32) + dskip_ref[...] * u_ref[...].astype(F32)
    a = 0.5 * s * (1.0 + jnp.tanh(math.sqrt(2.0 / math.pi) * (s + 0.044715 * (s * s * s))))
    vg = _dot(a.astype(BF16), wglu_ref[...])
    y_a = vg[:, :D_MODEL] * jax.nn.sigmoid(vg[:, D_MODEL:])
    y_b = _dot(mx, wfnet_ref[...])
    merged = (jax.nn.sigmoid(za_ref[...].astype(F32)) * y_a
              + jax.nn.sigmoid(zb_ref[...].astype(F32)) * y_b)
    x = x + _dot(merged.astype(BF16), wout_ref[...])
    h = _rms(x, gffn_ref[...]).astype(BF16)
    up = jnp.maximum(_dot(h, wup_ref[...]), 0.0)
    x = x + _dot((up * up).astype(BF16), wdown_ref[...])
    h = _rms(x, gple_ref[...]).astype(BF16)
    gate = jax.nn.sigmoid(_dot(h, wgate_ref[...]))
    x = x + gate * _dot(p_ref[...].astype(BF16), wple_ref[...])
    o_ref[...] = _rms(x, gfin_ref[...]) if final_norm else x


def _stage_d(x2, u, y, mx0, mx1, za, zb, p2, consts, final_norm):
    n = x2.shape[0]
    half = n // TM_D // 2
    row = lambda w: pl.BlockSpec((TM_D, w), lambda i: (i, 0))
    mx0_spec = pl.BlockSpec((TM_D, FFT_WIDTH), lambda i: (jnp.minimum(i, half - 1), 0))
    mx1_spec = pl.BlockSpec((TM_D, FFT_WIDTH), lambda i: (jnp.maximum(i - half, 0), 0))
    const = lambda a: pl.BlockSpec(a.shape, lambda i: (0, 0), pipeline_mode=pl.Buffered(1))
    return pl.pallas_call(
        functools.partial(_stage_d_kernel, final_norm=final_norm, half_tiles=half),
        grid=(n // TM_D,),
        in_specs=[row(D_MODEL), row(SSM_WIDTH), row(SSM_WIDTH), mx0_spec, mx1_spec,
                  row(D_MODEL), row(D_MODEL), row(D_PLE)] + [const(c) for c in consts],
        out_specs=row(D_MODEL),
        out_shape=jax.ShapeDtypeStruct((n, D_MODEL), F32),
        compiler_params=pltpu.CompilerParams(
            dimension_semantics=("parallel",), vmem_limit_bytes=VMEM_LIMIT),
        name="stage_d_merge_mlp",
    )(x2, u, y, mx0, mx1, za, zb, p2, *consts)


def kernel(x, p, g_mix, w_in, lam_re, lam_im, log_dt, b_re, b_im, c_re, c_im, d_skip,
           w_glu, w_fnet, w_out, g_ffn, w_up, w_down, g_ple, w_ple_gate, w_ple, g_final):
    depth = w_in.shape[0]
    bsz, seq, d = x.shape
    assert d == D_MODEL and seq % CHUNK == 0 and (bsz * seq) % TM_A == 0
    assert bsz == 2 * GROUPS_PER_TILE
    n_chunks = seq // CHUNK
    n = bsz * seq
    assert seq % TM_A == 0 and TM_A % (2 * GROUPS_PER_TILE * SEQ_RADIX) == 0
    w_ch, w_seq, twc, tws = (jnp.asarray(t) for t in _dft_tables(seq))
    w_ch, w_seq = w_ch.astype(BF16), w_seq.astype(BF16)
    vec = lambda v: v.reshape(1, -1).astype(F32)

    x2 = x.reshape(n, d)
    for i in range(depth):
        u, pf, za, zb, glu_b, fnet_b, out_b, up_b, down_b, gate_b = _stage_a(
            x2, vec(g_mix[i]), w_in[i], w_ch,
            [w_glu[i], w_fnet[i], w_out[i], w_up[i], w_down[i], w_ple_gate[i]], bsz, seq)

        hb = bsz // 2
        u_sb = u.reshape(bsz, seq, SSM_WIDTH).transpose(1, 0, 2).reshape(n, SSM_WIDTH)
        mx0 = _fnet_call(pf, w_seq, twc, tws, 0, hb, u)
        y_sb = _s5_call(u_sb, _s5_params(lam_re[i], lam_im[i], log_dt[i], b_re[i], b_im[i],
                                         c_re[i], c_im[i]), mx0, n_chunks, bsz)
        mx1 = _fnet_call(pf, w_seq, twc, tws, hb, bsz - hb, y_sb)
        y = y_sb.reshape(seq, bsz, SSM_WIDTH).transpose(1, 0, 2).reshape(n, SSM_WIDTH)

        consts = [vec(d_skip[i]), glu_b, fnet_b, out_b, vec(g_ffn[i]), up_b, down_b,
                  vec(g_ple[i]), gate_b, w_ple[i].astype(BF16), vec(g_final)]
        x2 = _stage_d(x2, u, y, mx0.reshape(n // 2, FFT_WIDTH), mx1.reshape(n // 2, FFT_WIDTH), za, zb,
                      p[i].reshape(n, D_PLE), consts, i == depth - 1)
    return x2.reshape(bsz, seq, d)
```

```python
import functools
import math

import jax
import jax.numpy as jnp
import numpy as np
from jax import lax
from jax.experimental import pallas as pl
from jax.experimental.pallas import tpu as pltpu

D_MODEL = 1024
SSM_WIDTH = 512
SSM_GROUP = 16
SSM_GROUPS = 32
SSM_STATE = 64
FFT_WIDTH = 512
FFT_GROUPS = 4
FFT_GROUP = 128
D_FF = 4096
D_PLE = 256
EPS = 1e-6

CHUNK = 16
CHUNK_VEC = CHUNK * SSM_GROUP
SEQ_RADIX = 8
LANES = 128

TM_A = 1024
TM_D = 512
FNET_BATCH = 1
VMEM_LIMIT = 56 * 1024 * 1024
S5_VMEM_LIMIT = 60 * 1024 * 1024

F32 = jnp.float32
BF16 = jnp.bfloat16
U32 = jnp.uint32


def _rms(x, g):
    ms = jnp.mean(x * x, axis=-1, keepdims=True)
    return (x * lax.rsqrt(ms + EPS)) * g


def _dot(a, b):
    return jnp.dot(a, b, preferred_element_type=F32)


def _dot_nt(a, b, precision=None):
    return lax.dot_general(a, b, (((1,), (1,)), ((), ())), precision=precision,
                           preferred_element_type=F32)


def _stage_a_kernel(x_ref, g_ref, win_ref, wdft_ref, *rest, n_cast):
    cast_in, rest = rest[:n_cast], rest[n_cast:]
    u_ref, pf_ref, za_ref, zb_ref = rest[:4]
    cast_out, z_ref, w_ref = rest[4:4 + n_cast], rest[4 + n_cast], rest[5 + n_cast]

    @pl.when(pl.program_id(0) == 0)
    def _():
        w_ref[...] = win_ref[...].astype(BF16)

    for src, dst in zip(cast_in, cast_out):
        dst[...] = src[...].astype(BF16)
    h = _rms(x_ref[...], g_ref[...]).astype(BF16)
    uf = _dot(h, w_ref[:, SSM_WIDTH:SSM_WIDTH + FFT_WIDTH]).astype(BF16)
    for gi in range(FFT_GROUPS):
        z = _dot(uf[:, gi * FFT_GROUP:(gi + 1) * FFT_GROUP], wdft_ref[...])
        z_ref[gi] = z[:, :FFT_GROUP]
        z_ref[FFT_GROUPS + gi] = z[:, FFT_GROUP:]
    for s2 in range(SEQ_RADIX):
        for part in range(2):
            for gi in range(FFT_GROUPS):
                zs = z_ref[part * FFT_GROUPS + gi, pl.ds(s2, TM_A // SEQ_RADIX, stride=SEQ_RADIX), :]
                pf_ref[0, s2, part, :, gi * FFT_GROUP:(gi + 1) * FFT_GROUP] = zs.astype(BF16)
    gates = SSM_WIDTH + FFT_WIDTH
    u_ref[...] = _dot(h, w_ref[:, :SSM_WIDTH]).astype(BF16)
    za_ref[...] = _dot(h, w_ref[:, gates:gates + D_MODEL]).astype(BF16)
    zb_ref[...] = _dot(h, w_ref[:, gates + D_MODEL:]).astype(BF16)


def _stage_a(x2, g_mix, w_in, w_dft, to_cast, bsz, seq):
    n = x2.shape[0]
    steps = n // TM_A
    tiles_per_seq = seq // TM_A
    seq_sub = seq // SEQ_RADIX
    row = lambda w: pl.BlockSpec((TM_A, w), lambda i: (i, 0))
    full = lambda a: pl.BlockSpec(a.shape, lambda i: (0,) * a.ndim)
    pf_spec = pl.BlockSpec((1, SEQ_RADIX, 2, TM_A // SEQ_RADIX, FFT_WIDTH),
                           lambda i: (i // tiles_per_seq, 0, 0, i % tiles_per_seq, 0))
    for w in to_cast:
        assert w.shape[0] % (steps * 2 * GROUPS_PER_TILE) == 0
    cast_specs = [pl.BlockSpec((w.shape[0] // steps, w.shape[1]), lambda i: (i, 0)) for w in to_cast]
    return pl.pallas_call(
        functools.partial(_stage_a_kernel, n_cast=len(to_cast)),
        grid=(steps,),
        in_specs=[row(D_MODEL), full(g_mix),
                  pl.BlockSpec(w_in.shape, lambda i: (0, 0), pipeline_mode=pl.Buffered(1)),
                  full(w_dft)] + cast_specs,
        out_specs=[row(SSM_WIDTH), pf_spec, row(D_MODEL), row(D_MODEL)] + cast_specs,
        out_shape=[jax.ShapeDtypeStruct((n, SSM_WIDTH), BF16),
                   jax.ShapeDtypeStruct((bsz, SEQ_RADIX, 2, seq_sub, FFT_WIDTH), BF16),
                   jax.ShapeDtypeStruct((n, D_MODEL), BF16),
                   jax.ShapeDtypeStruct((n, D_MODEL), BF16)]
                  + [jax.ShapeDtypeStruct(w.shape, BF16) for w in to_cast],
        scratch_shapes=[pltpu.VMEM((2 * FFT_GROUPS, TM_A, FFT_GROUP), F32), pltpu.VMEM(w_in.shape, BF16)],
        compiler_params=pltpu.CompilerParams(
            dimension_semantics=("arbitrary",), vmem_limit_bytes=VMEM_LIMIT),
        name="stage_a_inproj",
    )(x2, g_mix, w_in, w_dft, *to_cast)


GROUPS_PER_TILE = LANES // SSM_GROUP
SCAN_GROUPS = 4
RELAYOUT_UNROLL = 4


def _chunk_time_order(slot_shift):
    gpt = GROUPS_PER_TILE
    return [gpt * (pos // gpt) + (pos % gpt - slot_shift) % gpt for pos in range(CHUNK)]


def _group_operators(lam_re, lam_im, log_dt, btr, bti, cr, ci, slot_shift):
    order = _chunk_time_order(slot_shift)
    half = SSM_STATE
    is_f = lax.broadcasted_iota(jnp.int32, (1, 2 * half), 1) < half
    dt = jnp.exp(log_dt)
    lr, li = lam_re * dt, lam_im * dt
    mag = jnp.exp(lr)
    a1r, a1i = mag * jnp.cos(li), mag * jnp.sin(li)
    den = lam_re * lam_re + lam_im * lam_im
    kr = ((a1r - 1.0) * lam_re + a1i * lam_im) / den
    ki = (a1i * lam_re - (a1r - 1.0) * lam_im) / den
    bbr, bbi = kr * btr - ki * bti, kr * bti + ki * btr
    pw = [(jnp.ones_like(a1r), jnp.zeros_like(a1r))]
    for _ in range(CHUNK):
        pr, pi = pw[-1]
        pw.append((pr * a1r - pi * a1i, pr * a1i + pi * a1r))

    def dir_pick(seq, tf, tb):
        return jnp.where(is_f, seq[tf], seq[tb])

    pwr, pwi = [q[0] for q in pw], [q[1] for q in pw]
    m_rows = []
    for t in range(CHUNK):
        ar, ai = dir_pick(pwr, CHUNK - 1 - t, t), dir_pick(pwi, CHUNK - 1 - t, t)
        m_rows.append(jnp.concatenate([bbr * ar - bbi * ai, bbr * ai + bbi * ar], axis=1))
    m_mat = jnp.concatenate([m_rows[t] for t in order], axis=0)

    car = [cr * pwr[k] - ci * pwi[k] for k in range(CHUNK + 1)]
    cai = [cr * pwi[k] + ci * pwr[k] for k in range(CHUNK + 1)]
    n_rows = [jnp.concatenate([dir_pick(car, t + 1, CHUNK - t), -dir_pick(cai, t + 1, CHUNK - t)], axis=1)
              for t in range(CHUNK)]
    nt_mat = jnp.concatenate([n_rows[t] for t in order], axis=0)

    k_rows = []
    for i in range(2 * CHUNK - 1):
        lag = abs(i - (CHUNK - 1))
        if i == CHUNK - 1:
            rr, ii = car[lag], cai[lag]
        else:
            keep = is_f if i > CHUNK - 1 else jnp.logical_not(is_f)
            rr, ii = jnp.where(keep, car[lag], 0.0), jnp.where(keep, cai[lag], 0.0)
        k_rows.append(jnp.concatenate([rr, ii], axis=1))
    k_rows.append(jnp.zeros_like(k_rows[0]))
    ca_seq = jnp.concatenate(k_rows, axis=0)
    lb = jnp.concatenate([bbr, -bbi], axis=1)
    k_seq = _dot_nt(lb, ca_seq, precision=lax.Precision.HIGHEST)
    t_rows = []
    for t in range(CHUNK):
        off = SSM_GROUP * (CHUNK - 1 - t)
        t_rows.append(k_seq[:, off:off + CHUNK_VEC])
    t_mat = jnp.concatenate([t_rows[t] for t in order], axis=0)
    if slot_shift:
        t_mat = jnp.concatenate(
            [pltpu.roll(t_mat[:, k * LANES:(k + 1) * LANES], slot_shift * SSM_GROUP, 1)
             for k in range(CHUNK_VEC // LANES)], axis=1)
    return t_mat.astype(BF16), m_mat.astype(BF16), nt_mat.astype(BF16), pwr[CHUNK], pwi[CHUNK]


def _s5_kernel(u_ref, prm_ref, after_ref, y_ref,
               ug_ref, e_ref, x_ref, t_ref, nt_ref, *, n_chunks, bsz):
    del after_ref
    half = SSM_STATE
    gpt = GROUPS_PER_TILE
    slot = lax.broadcasted_iota(jnp.int32, (bsz, LANES), 1) // SSM_GROUP
    slot_is = [slot == s for s in range(gpt)]

    def rot(v, slots):
        if slots % gpt == 0:
            return v
        return pltpu.roll(v.astype(F32), (slots % gpt) * SSM_GROUP, 1).astype(BF16)

    def gather_chunk(c, carry):
        base = c * CHUNK
        r = [rot(u_ref[pl.ds(pl.multiple_of((base + t) * bsz, bsz), bsz), :], t) for t in range(CHUNK)]
        for tile in range(CHUNK // gpt):
            for g in range(gpt):
                d = r[tile * gpt + (-g) % gpt]
                for s in range(1, gpt):
                    d = jnp.where(slot_is[s], r[tile * gpt + (s - g) % gpt], d)
                ug_ref[g, pl.ds(pl.multiple_of(c * bsz, bsz), bsz), tile * LANES:(tile + 1) * LANES] = d
        return carry

    lax.fori_loop(0, n_chunks, gather_chunk, 0, unroll=RELAYOUT_UNROLL)

    is_fwd = lax.broadcasted_iota(jnp.int32, (bsz, 2 * half), 1) < half
    for first in range(0, gpt, SCAN_GROUPS):
        a16 = []
        for k in range(SCAN_GROUPS):
            g = first + k
            t_mat, m_mat, nt_mat, ar, ai = _group_operators(
                *(prm_ref[g, r0:r0 + nr, :] for r0, nr in _S5_PARAM_ROWS), g)
            t_ref[k] = t_mat
            nt_ref[k] = nt_mat
            a16.append((ar, ai))
            e_ref[k] = _dot(ug_ref[g], m_mat)

        def scan_step(i, carry):
            rf = pl.multiple_of(i * bsz, bsz)
            rb = pl.multiple_of((n_chunks - 1 - i) * bsz, bsz)
            out = []
            for k in range(SCAN_GROUPS):
                xr, xi = carry[k]
                ar, ai = a16[k]
                xrb, xib = xr.astype(BF16), xi.astype(BF16)
                x_ref[k, pl.ds(rf, bsz), 0:half] = xrb[:, 0:half]
                x_ref[k, pl.ds(rb, bsz), half:2 * half] = xrb[:, half:]
                x_ref[k, pl.ds(rf, bsz), 2 * half:3 * half] = xib[:, 0:half]
                x_ref[k, pl.ds(rb, bsz), 3 * half:] = xib[:, half:]
                er = jnp.where(is_fwd, e_ref[k, pl.ds(rf, bsz), 0:2 * half],
                               e_ref[k, pl.ds(rb, bsz), 0:2 * half])
                ei = jnp.where(is_fwd, e_ref[k, pl.ds(rf, bsz), 2 * half:],
                               e_ref[k, pl.ds(rb, bsz), 2 * half:])
                out.append((ar * xr - ai * xi + er, ar * xi + ai * xr + ei))
            return tuple(out)

        zero = jnp.zeros((bsz, 2 * half), F32)
        lax.fori_loop(0, n_chunks, scan_step, tuple((zero, zero) for _ in range(SCAN_GROUPS)))

        for k in range(SCAN_GROUPS):
            g = first + k
            y = _dot(ug_ref[g], t_ref[k]) + _dot_nt(x_ref[k], nt_ref[k])
            ug_ref[g] = y.astype(BF16)

    def scatter_chunk(c, carry):
        rows = pl.ds(pl.multiple_of(c * bsz, bsz), bsz)
        base = c * CHUNK
        for tile in range(CHUNK // gpt):
            d = [ug_ref[g, rows, tile * LANES:(tile + 1) * LANES] for g in range(gpt)]
            for i in range(gpt):
                w = d[(-i) % gpt]
                for s in range(1, gpt):
                    w = jnp.where(slot_is[s], d[(s - i) % gpt], w)
                t = tile * gpt + i
                y_ref[pl.ds(pl.multiple_of((base + t) * bsz, bsz), bsz), :] = rot(w, -i)
        return carry

    lax.fori_loop(0, n_chunks, scatter_chunk, 0, unroll=RELAYOUT_UNROLL)


def _s5_call(u_sb, params, after, n_chunks, bsz):
    n_tok, width = u_sb.shape
    rows = n_chunks * bsz
    gpt = GROUPS_PER_TILE
    return pl.pallas_call(
        functools.partial(_s5_kernel, n_chunks=n_chunks, bsz=bsz),
        grid=(width // LANES,),
        in_specs=[pl.BlockSpec((n_tok, LANES), lambda j: (0, j)),
                  pl.BlockSpec((gpt,) + params.shape[1:], lambda j: (j, 0, 0)),
                  pl.BlockSpec(memory_space=pl.ANY)],
        out_specs=pl.BlockSpec((n_tok, LANES), lambda j: (0, j)),
        out_shape=jax.ShapeDtypeStruct((n_tok, width), BF16),
        scratch_shapes=[pltpu.VMEM((gpt, rows, CHUNK_VEC), BF16),
                        pltpu.VMEM((SCAN_GROUPS, rows, 4 * SSM_STATE), F32),
                        pltpu.VMEM((SCAN_GROUPS, rows, 4 * SSM_STATE), BF16),
                        pltpu.VMEM((SCAN_GROUPS, CHUNK_VEC, CHUNK_VEC), BF16),
                        pltpu.VMEM((SCAN_GROUPS, CHUNK_VEC, CHUNK_VEC), BF16)],
        compiler_params=pltpu.CompilerParams(
            dimension_semantics=("parallel",), vmem_limit_bytes=S5_VMEM_LIMIT),
        name="s5_chunked_scan",
    )(u_sb, params, after)


_S5_PARAM_ROWS = ((0, 1), (1, 1), (2, 1), (8, 16), (24, 16), (40, 16), (56, 16))


def _s5_params(lam_re, lam_im, log_dt, b_re, b_im, c_re, c_im):
    cat = lambda a: jnp.concatenate([a[0], a[1]], axis=-1)
    g = lam_re.shape[1]
    ldt = jnp.broadcast_to(log_dt[:, :, None], (2, g, SSM_STATE))
    head = jnp.stack([cat(lam_re), cat(lam_im), cat(ldt)], axis=1)
    head = jnp.pad(head, ((0, 0), (0, 5), (0, 0)))
    return jnp.concatenate([head, cat(b_re.transpose(0, 1, 3, 2)), cat(b_im.transpose(0, 1, 3, 2)),
                            cat(c_re), cat(c_im)], axis=1)


def _dft8_real(a):
    def dft4(b):
        (a0r, a0i), (a1r, a1i), (a2r, a2i), (a3r, a3i) = b
        t0r, t0i, t1r, t1i = a0r + a2r, a0i + a2i, a0r - a2r, a0i - a2i
        t2r, t2i, t3r, t3i = a1r + a3r, a1i + a3i, a1r - a3r, a1i - a3i
        return [(t0r + t2r, t0i + t2i), (t1r + t3i, t1i - t3r), (t0r - t2r, t0i - t2i), (t1r - t3i, t1i + t3r)]

    ev, od = dft4(a[0::2]), dft4(a[1::2])
    rt = math.sqrt(0.5)
    p = [od[0][0], rt * (od[1][0] + od[1][1]), od[2][1], rt * (od[3][1] - od[3][0])]
    return [ev[k][0] + p[k] for k in range(4)] + [ev[k][0] - p[k] for k in range(4)]


def _fnet_kernel(p_ref, w1_ref, twc_ref, tws_ref, after_ref, o_ref):
    del after_ref
    sub = w1_ref.shape[0] // 2
    width = 2 * LANES
    for b in range(p_ref.shape[0]):
        for j in range(FFT_WIDTH // width):
            cols = slice(j * width, (j + 1) * width)
            a = []
            for s2 in range(SEQ_RADIX):
                y = _dot(w1_ref[...], p_ref[b, s2, :, :, cols].reshape(2 * sub, width))
                yr, yi = y[:sub], y[sub:]
                if s2:
                    c = jnp.concatenate([twc_ref[s2]] * (width // LANES), axis=1)
                    s = jnp.concatenate([tws_ref[s2]] * (width // LANES), axis=1)
                    yr, yi = c * yr + s * yi, c * yi - s * yr
                a.append((yr, yi))
            out = _dft8_real(a)
            for m2 in range(SEQ_RADIX):
                o_ref[b, m2 * sub:(m2 + 1) * sub, cols] = out[m2].astype(BF16)


def _fnet_call(pf5, w1, twc, tws, first, count, after):
    _, radix, _, sub, width = pf5.shape
    assert first % FNET_BATCH == 0 and count % FNET_BATCH == 0
    const = lambda a: pl.BlockSpec(a.shape, lambda i: (0,) * a.ndim)
    return pl.pallas_call(
        _fnet_kernel,
        grid=(count // FNET_BATCH,),
        in_specs=[pl.BlockSpec((FNET_BATCH, radix, 2, sub, width),
                               lambda i: (i + first // FNET_BATCH, 0, 0, 0, 0)),
                  const(w1), const(twc), const(tws), pl.BlockSpec(memory_space=pl.ANY)],
        out_specs=pl.BlockSpec((FNET_BATCH, radix * sub, width), lambda i: (i, 0, 0)),
        out_shape=jax.ShapeDtypeStruct((count, radix * sub, width), BF16),
        compiler_params=pltpu.CompilerParams(
            dimension_semantics=("parallel",), vmem_limit_bytes=VMEM_LIMIT),
        name="fnet_seq_dft",
    )(pf5, w1, twc, tws, after)


@functools.lru_cache(maxsize=None)
def _dft_tables(seq):
    c = np.arange(FFT_GROUP)
    ang = 2.0 * np.pi * np.outer(c, c) / FFT_GROUP
    w_ch = np.concatenate([np.cos(ang), -np.sin(ang)], axis=1) / math.sqrt(FFT_GROUP)
    sub = seq // SEQ_RADIX
    m = np.arange(sub)
    ang = 2.0 * np.pi * (np.outer(m, m) % sub) / sub
    cs, sn = np.cos(ang) / math.sqrt(seq), np.sin(ang) / math.sqrt(seq)
    w1 = np.block([[cs, sn], [-sn, cs]])
    ang = 2.0 * np.pi * np.outer(np.arange(SEQ_RADIX), m) / seq
    rep = lambda t: np.repeat(t[:, :, None], LANES, axis=2).astype(np.float32)
    return w_ch.astype(np.float32), w1.astype(np.float32), rep(np.cos(ang)), rep(np.sin(ang))


def _stage_d_kernel(x_ref, u_ref, y_ref, mx0_ref, mx1_ref, za_ref, zb_ref, p_ref,
                    dskip_ref, wglu_ref, wfnet_ref, wout_ref, gffn_ref, wup_ref, wdown_ref,
                    gple_ref, wgate_ref, wple_ref, gfin_ref, o_ref, *, final_norm, half_tiles):
    mx = jnp.where(pl.program_id(0) < half_tiles, mx0_ref[...], mx1_ref[...])
    x = x_ref[...]
    s = y_ref[...].astype(F32) + dskip_ref[...] * u_ref[...].astype(F32)
    a = 0.5 * s * (1.0 + jnp.tanh(math.sqrt(2.0 / math.pi) * (s + 0.044715 * (s * s * s))))
    vg = _dot(a.astype(BF16), wglu_ref[...])
    y_a = vg[:, :D_MODEL] * jax.nn.sigmoid(vg[:, D_MODEL:])
    y_b = _dot(mx, wfnet_ref[...])
    merged = (jax.nn.sigmoid(za_ref[...].astype(F32)) * y_a
              + jax.nn.sigmoid(zb_ref[...].astype(F32)) * y_b)
    x = x + _dot(merged.astype(BF16), wout_ref[...])
    h = _rms(x, gffn_ref[...]).astype(BF16)
    up = jnp.maximum(_dot(h, wup_ref[...]), 0.0)
    x = x + _dot((up * up).astype(BF16), wdown_ref[...])
    h = _rms(x, gple_ref[...]).astype(BF16)
    gate = jax.nn.sigmoid(_dot(h, wgate_ref[...]))
    x = x + gate * _dot(p_ref[...].astype(BF16), wple_ref[...])
    o_ref[...] = _rms(x, gfin_ref[...]) if final_norm else x


def _stage_d(x2, u, y, mx0, mx1, za, zb, p2, consts, final_norm):
    n = x2.shape[0]
    half = n // TM_D // 2
    row = lambda w: pl.BlockSpec((TM_D, w), lambda i: (i, 0))
    mx0_spec = pl.BlockSpec((TM_D, FFT_WIDTH), lambda i: (jnp.minimum(i, half - 1), 0))
    mx1_spec = pl.BlockSpec((TM_D, FFT_WIDTH), lambda i: (jnp.maximum(i - half, 0), 0))
    const = lambda a: pl.BlockSpec(a.shape, lambda i: (0, 0), pipeline_mode=pl.Buffered(1))
    return pl.pallas_call(
        functools.partial(_stage_d_kernel, final_norm=final_norm, half_tiles=half),
        grid=(n // TM_D,),
        in_specs=[row(D_MODEL), row(SSM_WIDTH), row(SSM_WIDTH), mx0_spec, mx1_spec,
                  row(D_MODEL), row(D_MODEL), row(D_PLE)] + [const(c) for c in consts],
        out_specs=row(D_MODEL),
        out_shape=jax.ShapeDtypeStruct((n, D_MODEL), F32),
        compiler_params=pltpu.CompilerParams(
            dimension_semantics=("parallel",), vmem_limit_bytes=VMEM_LIMIT),
        name="stage_d_merge_mlp",
    )(x2, u, y, mx0, mx1, za, zb, p2, *consts)


def kernel(x, p, g_mix, w_in, lam_re, lam_im, log_dt, b_re, b_im, c_re, c_im, d_skip,
           w_glu, w_fnet, w_out, g_ffn, w_up, w_down, g_ple, w_ple_gate, w_ple, g_final):
    depth = w_in.shape[0]
    bsz, seq, d = x.shape
    assert d == D_MODEL and seq % CHUNK == 0 and (bsz * seq) % TM_A == 0
    assert bsz == 2 * GROUPS_PER_TILE
    n_chunks = seq // CHUNK
    n = bsz * seq
    assert seq % TM_A == 0 and TM_A % (2 * GROUPS_PER_TILE * SEQ_RADIX) == 0
    w_ch, w_seq, twc, tws = (jnp.asarray(t) for t in _dft_tables(seq))
    w_ch, w_seq = w_ch.astype(BF16), w_seq.astype(BF16)
    vec = lambda v: v.reshape(1, -1).astype(F32)

    x2 = x.reshape(n, d)
    for i in range(depth):
        u, pf, za, zb, glu_b, fnet_b, out_b, up_b, down_b, gate_b = _stage_a(
            x2, vec(g_mix[i]), w_in[i], w_ch,
            [w_glu[i], w_fnet[i], w_out[i], w_up[i], w_down[i], w_ple_gate[i]], bsz, seq)

        hb = bsz // 2
        u_sb = u.reshape(bsz, seq, SSM_WIDTH).transpose(1, 0, 2).reshape(n, SSM_WIDTH)
        mx0 = _fnet_call(pf, w_seq, twc, tws, 0, hb, u)
        y_sb = _s5_call(u_sb, _s5_params(lam_re[i], lam_im[i], log_dt[i], b_re[i], b_im[i],
                                         c_re[i], c_im[i]), mx0, n_chunks, bsz)
        mx1 = _fnet_call(pf, w_seq, twc, tws, hb, bsz - hb, y_sb)
        y = y_sb.reshape(seq, bsz, SSM_WIDTH).transpose(1, 0, 2).reshape(n, SSM_WIDTH)

        consts = [vec(d_skip[i]), glu_b, fnet_b, out_b, vec(g_ffn[i]), up_b, down_b,
                  vec(g_ple[i]), gate_b, w_ple[i].astype(BF16), vec(g_final)]
        x2 = _stage_d(x2, u, y, mx0.reshape(n // 2, FFT_WIDTH), mx1.reshape(n // 2, FFT_WIDTH), za, zb,
                      p[i].reshape(n, D_PLE), consts, i == depth - 1)
    return x2.reshape(bsz, seq, d)
```
